```python
import math
import jax, jax.numpy as jnp
from jax import lax
import numpy as np

D_MODEL = 1024
BATCH = 4
SEQ = 8192
DEPTH = 1

D_FF = 2816
FFN_RES = 0.5
GLA_HEADS = 4
GLA_DK = 128
GLA_DV = 256
GLA_RANK = 16
GLA_TAU = 16.0
DN_HEADS = 8
DN_DK = 128
DN_DV = 128
CONV_K = 4
CHUNK = 64
EPS = 1e-6

GLA_QK = GLA_HEADS * GLA_DK
GLA_V = GLA_HEADS * GLA_DV
DN_QK = DN_HEADS * DN_DK
DN_V = DN_HEADS * DN_DV
IN_SIZES = (GLA_QK, GLA_QK, GLA_V, GLA_V, GLA_RANK,
            DN_QK, DN_QK, DN_V, DN_V, DN_HEADS, DN_HEADS,
            D_MODEL, D_MODEL)
D_IN = sum(IN_SIZES)

kernel_name = "macaron_gla_gdn_gated_merge"


def rms_norm(x, w):
    xf = x.astype(jnp.float32)
    xf = xf * lax.rsqrt(jnp.mean(xf * xf, axis=-1, keepdims=True) + EPS)
    return xf.astype(x.dtype) * w


def head_rms_norm(o, w, heads):
    b, t, _ = o.shape
    oh = o.reshape(b, t, heads, -1)
    oh = oh * lax.rsqrt(jnp.mean(oh * oh, axis=-1, keepdims=True) + EPS) * w.astype(jnp.float32)
    return oh.reshape(b, t, -1)


def swiglu(h, w_gate, w_up, w_down):
    return (jax.nn.silu(h @ w_gate) * (h @ w_up)) @ w_down


def to_chunks(t, heads):
    b, T, _ = t.shape
    return t.reshape(b, T // CHUNK, CHUNK, heads, -1).transpose(0, 3, 1, 2, 4)


def scalar_chunks(t):
    b, T, h = t.shape
    return t.reshape(b, T // CHUNK, CHUNK, h).transpose(0, 3, 1, 2)


def from_chunks(t):
    b, h, n, c, d = t.shape
    return t.transpose(0, 2, 3, 1, 4).reshape(b, n * c, h * d)


def causal_depthwise_conv(x, w):
    c = x.shape[-1]
    return lax.conv_general_dilated(
        x, w[:, None, :], window_strides=(1,), padding=[(CONV_K - 1, 0)],
        dimension_numbers=("NWC", "WIO", "NWC"), feature_group_count=c)


def gla_attention(q, k, v, log_a):
    f32 = jnp.float32
    q = to_chunks(q.astype(f32), GLA_HEADS) * GLA_DK ** -0.5
    k = to_chunks(k.astype(f32), GLA_HEADS)
    v = to_chunks(v.astype(f32), GLA_HEADS)
    b = jnp.cumsum(to_chunks(log_a.astype(f32), GLA_HEADS), axis=3)
    b_last = b[:, :, :, -1:, :]
    causal = jnp.tril(jnp.ones((CHUNK, CHUNK), bool))
    q_in = q * jnp.exp(b)
    scores = jnp.einsum("bhnid,bhnjd->bhnij", q_in, k * jnp.exp(-b))
    scores = jnp.where(causal, scores, 0.0)
    o_intra = jnp.einsum("bhnij,bhnjv->bhniv", scores, v)
    k_state = k * jnp.exp(b_last - b)
    a_chunk = jnp.exp(b_last[:, :, :, 0, :])

    def step(S, xs):
        qc, kc, vc, ac = xs
        o = jnp.einsum("bhcd,bhdv->bhcv", qc, S)
        S = S * ac[..., None] + jnp.einsum("bhcd,bhcv->bhdv", kc, vc)
        return S, o

    bsz = q.shape[0]
    S0 = jnp.zeros((bsz, GLA_HEADS, GLA_DK, GLA_DV), f32)
    xs = (jnp.moveaxis(q_in, 2, 0), jnp.moveaxis(k_state, 2, 0),
          jnp.moveaxis(v, 2, 0), jnp.moveaxis(a_chunk, 2, 0))
    _, o_inter = lax.scan(step, S0, xs)
    return from_chunks(o_intra + jnp.moveaxis(o_inter, 0, 2))


def gated_delta_attention(q, k, v, g, beta):
    f32 = jnp.float32
    q = to_chunks(q.astype(f32), DN_HEADS)
    k = to_chunks(k.astype(f32), DN_HEADS)
    q = q * lax.rsqrt(jnp.sum(q * q, -1, keepdims=True) + EPS) * DN_DK ** -0.5
    k = k * lax.rsqrt(jnp.sum(k * k, -1, keepdims=True) + EPS)
    v = to_chunks(v.astype(f32), DN_HEADS)
    G = jnp.cumsum(scalar_chunks(g.astype(f32)), axis=-1)
    beta = scalar_chunks(beta.astype(f32))[..., None]
    causal = jnp.tril(jnp.ones((CHUNK, CHUNK), bool))
    strict = jnp.tril(jnp.ones((CHUNK, CHUNK), bool), k=-1)
    decay = jnp.exp(jnp.where(causal, G[..., :, None] - G[..., None, :], -jnp.inf))
    k_beta = k * beta
    kk = jnp.einsum("bhnid,bhnjd->bhnij", k_beta, k) * decay
    M = jnp.eye(CHUNK, dtype=f32) + jnp.where(strict, kk, 0.0)
    rhs = jnp.concatenate([v * beta, k_beta * jnp.exp(G)[..., None]], axis=-1)
    sol = lax.linalg.triangular_solve(M, rhs, left_side=True, lower=True, unit_diagonal=True)
    u, w = sol[..., :DN_DV], sol[..., DN_DV:]
    qk = jnp.einsum("bhnid,bhnjd->bhnij", q, k) * decay
    q_dec = q * jnp.exp(G)[..., None]
    G_last = G[..., -1:]
    k_state = k * jnp.exp(G_last - G)[..., None]
    g_chunk = jnp.exp(G_last[..., 0])

    def step(S, xs):
        qdc, qkc, uc, wc, kc, gc = xs
        v_new = uc - jnp.einsum("bhcd,bhdv->bhcv", wc, S)
        o = jnp.einsum("bhcd,bhdv->bhcv", qdc, S) + jnp.einsum("bhij,bhjv->bhiv", qkc, v_new)
        S = S * gc[..., None, None] + jnp.einsum("bhcd,bhcv->bhdv", kc, v_new)
        return S, o

    bsz = q.shape[0]
    S0 = jnp.zeros((bsz, DN_HEADS, DN_DK, DN_DV), f32)
    xs = tuple(jnp.moveaxis(t, 2, 0) for t in (q_dec, qk, u, w, k_state, g_chunk))
    _, o = lax.scan(step, S0, xs)
    return from_chunks(jnp.moveaxis(o, 0, 2))


def hybrid_mixer(h, w_in, w_gla_gate, b_gla_gate, conv_w, dn_a_log, dn_dt_bias,
                 gla_head_norm, dn_head_norm, w_out):
    proj = h @ w_in
    cuts = [int(c) for c in np.cumsum(IN_SIZES)[:-1]]
    (gq, gk, gv, gr, glr, dq, dk, dv, dgate, dbeta, da, merge_a, merge_b) = jnp.split(proj, cuts, axis=-1)
    log_a = jax.nn.log_sigmoid(glr @ w_gla_gate + b_gla_gate).astype(jnp.float32) / GLA_TAU
    o_a = gla_attention(gq, gk, gv, log_a)
    o_a = head_rms_norm(o_a, gla_head_norm, GLA_HEADS) * jax.nn.silu(gr)
    qkv = jax.nn.silu(causal_depthwise_conv(jnp.concatenate([dq, dk, dv], axis=-1), conv_w))
    dq, dk, dv = qkv[..., :DN_QK], qkv[..., DN_QK:2 * DN_QK], qkv[..., 2 * DN_QK:]
    g = -jnp.exp(dn_a_log.astype(jnp.float32)) * jax.nn.softplus((da + dn_dt_bias).astype(jnp.float32))
    beta = jax.nn.sigmoid(dbeta)
    o_b = gated_delta_attention(dq, dk, dv, g, beta)
    o_b = head_rms_norm(o_b, dn_head_norm, DN_HEADS) * jax.nn.silu(dgate)
    y = jax.nn.sigmoid(merge_a) * o_a + jax.nn.sigmoid(merge_b) * o_b
    return y.astype(h.dtype) @ w_out


def setup_inputs(seed: int = 0) -> dict:
    key = jax.random.key(seed)
    ks = jax.random.split(key, 24)
    L = DEPTH
    f32 = jnp.float32

    def dense(k, fan_in, shape):
        return jax.random.normal(k, shape, f32) * fan_in ** -0.5

    def gain(k, shape):
        return 1.0 + 0.02 * jax.random.normal(k, shape, f32)

    dt = jnp.exp(jax.random.uniform(ks[11], (L, DN_HEADS), f32, math.log(1e-3), math.log(1e-1)))
    return {
        "x": jax.random.normal(ks[0], (BATCH, SEQ, D_MODEL), f32),
        "ffn1_norm": gain(ks[1], (L, D_MODEL)),
        "ffn1_w_gate": dense(ks[2], D_MODEL, (L, D_MODEL, D_FF)),
        "ffn1_w_up": dense(ks[3], D_MODEL, (L, D_MODEL, D_FF)),
        "ffn1_w_down": dense(ks[4], D_FF, (L, D_FF, D_MODEL)),
        "mix_norm": gain(ks[5], (L, D_MODEL)),
        "w_in": dense(ks[6], D_MODEL, (L, D_MODEL, D_IN)),
        "w_gla_gate": dense(ks[7], GLA_RANK, (L, GLA_RANK, GLA_QK)),
        "b_gla_gate": 0.1 * jax.random.normal(ks[8], (L, GLA_QK), f32),
        "conv_w": dense(ks[9], CONV_K, (L, CONV_K, 2 * DN_QK + DN_V)),
        "dn_a_log": jnp.log(jax.random.uniform(ks[10], (L, DN_HEADS), f32, 1.0, 16.0)),
        "dn_dt_bias": dt + jnp.log(-jnp.expm1(-dt)),
        "gla_head_norm": gain(ks[12], (L, GLA_DV)),
        "dn_head_norm": gain(ks[13], (L, DN_DV)),
        "w_out": dense(ks[14], D_MODEL, (L, D_MODEL, D_MODEL)),
        "ffn2_norm": gain(ks[15], (L, D_MODEL)),
        "ffn2_w_gate": dense(ks[16], D_MODEL, (L, D_MODEL, D_FF)),
        "ffn2_w_up": dense(ks[17], D_MODEL, (L, D_MODEL, D_FF)),
        "ffn2_w_down": dense(ks[18], D_FF, (L, D_FF, D_MODEL)),
        "final_norm": gain(ks[19], (D_MODEL,)),
    }


def reference(x, ffn1_norm, ffn1_w_gate, ffn1_w_up, ffn1_w_down, mix_norm, w_in,
              w_gla_gate, b_gla_gate, conv_w, dn_a_log, dn_dt_bias, gla_head_norm,
              dn_head_norm, w_out, ffn2_norm, ffn2_w_gate, ffn2_w_up, ffn2_w_down,
              final_norm):
    h = x
    for layer in range(DEPTH):
        h = h + FFN_RES * swiglu(rms_norm(h, ffn1_norm[layer]), ffn1_w_gate[layer],
                                 ffn1_w_up[layer], ffn1_w_down[layer])
        h = h + hybrid_mixer(rms_norm(h, mix_norm[layer]), w_in[layer], w_gla_gate[layer],
                             b_gla_gate[layer], conv_w[layer], dn_a_log[layer], dn_dt_bias[layer],
                             gla_head_norm[layer], dn_head_norm[layer], w_out[layer])
        h = h + FFN_RES * swiglu(rms_norm(h, ffn2_norm[layer]), ffn2_w_gate[layer],
                                 ffn2_w_up[layer], ffn2_w_down[layer])
    return rms_norm(h, final_norm)
```

```python
import functools

import jax
import jax.numpy as jnp
from jax import lax
from jax.experimental import pallas as pl
from jax.experimental.pallas import tpu as pltpu

F32 = jnp.float32
BF16 = jnp.bfloat16

D_MODEL = 1024
D_FF = 2816
FFN_RES = 0.5
GLA_HEADS = 4
GLA_DK = 128
GLA_DV = 256
GLA_RANK = 16
GLA_TAU = 16.0
DN_HEADS = 8
DN_DK = 128
DN_DV = 128
CONV_K = 4
CHUNK = 64
EPS = 1e-6

GLA_QK = GLA_HEADS * GLA_DK
GLA_V = GLA_HEADS * GLA_DV
DN_QK = DN_HEADS * DN_DK
DN_V = DN_HEADS * DN_DV

SMALL_W = 128
SMALL_T_ROWS = 32
LANE_GLR = 0
LANE_BETA = GLA_RANK
LANE_DA = GLA_RANK + DN_HEADS

FFN_TM = 1024
FFN_TF = 256
PROJ_TM = 1024
PROJ_TN = 1024
MIX_TT = 256
OUT_TM = 1024
CONV_HALO = 8

VMEM_LIMIT_BYTES = 48 * 1024 * 1024


def _cparams(sem):
    return pltpu.CompilerParams(dimension_semantics=sem, vmem_limit_bytes=VMEM_LIMIT_BYTES)


def _rms(x, w):
    return x * lax.rsqrt(jnp.mean(x * x, axis=-1, keepdims=True) + EPS) * w


def _sigmoid(x):
    return 1.0 / (1.0 + jnp.exp(-x))


def _silu(x):
    return x * _sigmoid(x)


def _softplus(x):
    return jnp.maximum(x, 0.0) + jnp.log1p(jnp.exp(-jnp.abs(x)))


def _dot(a, b):
    return jnp.dot(a, b, preferred_element_type=F32)


def _dot_nt(a, b):
    return lax.dot_general(a, b, (((1,), (1,)), ((), ())), preferred_element_type=F32)


def _dot_tn(a, b):
    return lax.dot_general(a, b, (((0,), (0,)), ((), ())), preferred_element_type=F32)


def _dot_hi(a, b):
    return jnp.dot(a, b, preferred_element_type=F32, precision=lax.Precision.HIGHEST)


def _ffn_body(x_ref, nw_ref, wg_ref, wu_ref, wd_ref, fw_ref, o_ref, xn_ref, *, final_norm):
    f = pl.program_id(1)

    @pl.when(f == 0)
    def _():
        xn_ref[...] = _rms(x_ref[...], nw_ref[...]).astype(BF16)
        o_ref[...] = jnp.zeros_like(o_ref)

    xn = xn_ref[...]
    g = _dot(xn, wg_ref[...])
    u = _dot(xn, wu_ref[...])
    a = (_silu(g) * u).astype(BF16)
    o_ref[...] += _dot(a, wd_ref[...])

    @pl.when(f == pl.num_programs(1) - 1)
    def _():
        h = x_ref[...] + FFN_RES * o_ref[...]
        if final_norm:
            h = _rms(h, fw_ref[...])
        o_ref[...] = h


def _ffn(x2d, norm_w, wg, wu, wd, final_w, final_norm):
    m, d = x2d.shape
    ff = wg.shape[1]
    tm, tf = min(FFN_TM, m), FFN_TF
    return pl.pallas_call(
        functools.partial(_ffn_body, final_norm=final_norm),
        grid=(m // tm, ff // tf),
        in_specs=[
            pl.BlockSpec((tm, d), lambda i, f: (i, 0)),
            pl.BlockSpec((1, d), lambda i, f: (0, 0)),
            pl.BlockSpec((d, tf), lambda i, f: (0, f)),
            pl.BlockSpec((d, tf), lambda i, f: (0, f)),
            pl.BlockSpec((tf, d), lambda i, f: (f, 0)),
            pl.BlockSpec((1, d), lambda i, f: (0, 0)),
        ],
        out_specs=pl.BlockSpec((tm, d), lambda i, f: (i, 0)),
        out_shape=jax.ShapeDtypeStruct((m, d), F32),
        scratch_shapes=[pltpu.VMEM((tm, d), BF16)],
        compiler_params=_cparams(("parallel", "arbitrary")),
        name="ffn_final" if final_norm else "ffn",
    )(x2d, norm_w, wg, wu, wd, final_w)


def _norm_to_scratch(x_ref, nw_ref, xn_ref):
    @pl.when(pl.program_id(1) == 0)
    def _():
        xn_ref[...] = _rms(x_ref[...], nw_ref[...]).astype(BF16)


def _proj_plain_body(x_ref, nw_ref, w_ref, o_ref, xn_ref):
    _norm_to_scratch(x_ref, nw_ref, xn_ref)
    o_ref[...] = _dot(xn_ref[...], w_ref[...]).astype(o_ref.dtype)


def _proj_gate_body(x_ref, nw_ref, wa_ref, wb_ref, o_ref, xn_ref):
    _norm_to_scratch(x_ref, nw_ref, xn_ref)
    xn = xn_ref[...]
    o_ref[...] = (_silu(_dot(xn, wa_ref[...])) * _sigmoid(_dot(xn, wb_ref[...]))).astype(o_ref.dtype)


def _proj_small_body(x_ref, nw_ref, w_ref, wt_ref, o_ref, ot_ref):
    xn = _rms(x_ref[...], nw_ref[...]).astype(BF16)
    o_ref[...] = _dot(xn, w_ref[...])
    ot_ref[0] = _dot_nt(wt_ref[...], xn)


def _proj_plain(h2d, norm_w, w):
    m, d = h2d.shape
    n = w.shape[1]
    tm, tn = min(PROJ_TM, m), PROJ_TN
    return pl.pallas_call(
        _proj_plain_body,
        grid=(m // tm, n // tn),
        in_specs=[
            pl.BlockSpec((tm, d), lambda i, j: (i, 0)),
            pl.BlockSpec((1, d), lambda i, j: (0, 0)),
            pl.BlockSpec((d, tn), lambda i, j: (0, j)),
        ],
        out_specs=pl.BlockSpec((tm, tn), lambda i, j: (i, j)),
        out_shape=jax.ShapeDtypeStruct((m, n), BF16),
        scratch_shapes=[pltpu.VMEM((tm, d), BF16)],
        compiler_params=_cparams(("parallel", "arbitrary")),
        name="proj_plain",
    )(h2d, norm_w, w)


def _proj_gate(h2d, norm_w, wa, wb):
    m, d = h2d.shape
    n = wa.shape[1]
    tm, tn = min(PROJ_TM, m), PROJ_TN
    return pl.pallas_call(
        _proj_gate_body,
        grid=(m // tm, n // tn),
        in_specs=[
            pl.BlockSpec((tm, d), lambda i, j: (i, 0)),
            pl.BlockSpec((1, d), lambda i, j: (0, 0)),
            pl.BlockSpec((d, tn), lambda i, j: (0, j)),
            pl.BlockSpec((d, tn), lambda i, j: (0, j)),
        ],
        out_specs=pl.BlockSpec((tm, tn), lambda i, j: (i, j)),
        out_shape=jax.ShapeDtypeStruct((m, n), BF16),
        scratch_shapes=[pltpu.VMEM((tm, d), BF16)],
        compiler_params=_cparams(("parallel", "arbitrary")),
        name="proj_gate",
    )(h2d, norm_w, wa, wb)


def _proj_small(h3d, norm_w, w, wt):
    b, t, d = h3d.shape
    tm = min(PROJ_TM, t)
    nt = t // tm
    h2d = h3d.reshape(b * t, d)
    return pl.pallas_call(
        _proj_small_body,
        grid=(b, nt),
        in_specs=[
            pl.BlockSpec((tm, d), lambda bi, ti: (bi * nt + ti, 0)),
            pl.BlockSpec((1, d), lambda bi, ti: (0, 0)),
            pl.BlockSpec((d, SMALL_W), lambda bi, ti: (0, 0)),
            pl.BlockSpec((SMALL_T_ROWS, d), lambda bi, ti: (0, 0)),
        ],
        out_specs=[
            pl.BlockSpec((tm, SMALL_W), lambda bi, ti: (bi * nt + ti, 0)),
            pl.BlockSpec((1, SMALL_T_ROWS, tm), lambda bi, ti: (bi, 0, ti)),
        ],
        out_shape=[
            jax.ShapeDtypeStruct((b * t, SMALL_W), F32),
            jax.ShapeDtypeStruct((b, SMALL_T_ROWS, t), F32),
        ],
        compiler_params=_cparams(("parallel", "parallel")),
        name="proj_small",
    )(h2d, norm_w, w, wt)


def _tri_masks():
    r = lax.broadcasted_iota(jnp.int32, (CHUNK, CHUNK), 0)
    c = lax.broadcasted_iota(jnp.int32, (CHUNK, CHUNK), 1)
    return r >= c, r > c, r <= c


def _gla_body(qk_ref, v_ref, sm_ref, gate_ref, wgg_ref, bgg_ref, hnw_ref, o_ref, st_ref):
    @pl.when(pl.program_id(1) == 0)
    def _():
        st_ref[...] = jnp.zeros_like(st_ref)

    causal, _, _ = _tri_masks()
    tril = jnp.where(causal, 1.0, 0.0).astype(F32)
    tt = qk_ref.shape[1]

    z = _dot(sm_ref[0].astype(BF16), wgg_ref[...]) + bgg_ref[...]
    log_a = (jnp.minimum(z, 0.0) - jnp.log1p(jnp.exp(-jnp.abs(z)))) * (1.0 / GLA_TAU)
    hnw = hnw_ref[...]

    for c in range(tt // CHUNK):
        rows = slice(c * CHUNK, (c + 1) * CHUNK)
        b = _dot_hi(tril, log_a[rows])
        b_last = b[CHUNK - 1:CHUNK, :]
        q = qk_ref[0, rows, 0:GLA_QK].astype(F32) * (GLA_DK ** -0.5)
        k = qk_ref[0, rows, GLA_QK:2 * GLA_QK].astype(F32)
        q_in = (q * jnp.exp(b)).astype(BF16)
        k_neg = (k * jnp.exp(-b)).astype(BF16)
        k_state = (k * jnp.exp(b_last - b)).astype(BF16)
        a_chunk = jnp.exp(b_last)
        for h in range(GLA_HEADS):
            kcols = slice(h * GLA_DK, (h + 1) * GLA_DK)
            vcols = slice(h * GLA_DV, (h + 1) * GLA_DV)
            qh, kh, ksh = q_in[:, kcols], k_neg[:, kcols], k_state[:, kcols]
            vh = v_ref[0, rows, vcols]
            st = st_ref[h]
            scores = jnp.where(causal, _dot_nt(qh, kh), 0.0)
            o = _dot(scores.astype(BF16), vh) + _dot_nt(qh, st.astype(BF16))
            st_ref[h] = st * a_chunk[:, kcols] + _dot_tn(vh, ksh)
            o = o * lax.rsqrt(jnp.mean(o * o, axis=-1, keepdims=True) + EPS) * hnw
            o_ref[0, rows, vcols] = (o * gate_ref[0, rows, vcols].astype(F32)).astype(o_ref.dtype)


def _gla(plain, small, gates, wgg, bgg, hnw, b, t):
    tt = min(MIX_TT, t)
    return pl.pallas_call(
        _gla_body,
        grid=(b, t // tt),
        in_specs=[
            pl.BlockSpec((1, tt, 2 * GLA_QK), lambda bi, ti: (bi, ti, 0)),
            pl.BlockSpec((1, tt, GLA_V), lambda bi, ti: (bi, ti, 1)),
            pl.BlockSpec((1, tt, SMALL_W), lambda bi, ti: (bi, ti, 0)),
            pl.BlockSpec((1, tt, GLA_V), lambda bi, ti: (bi, ti, 0)),
            pl.BlockSpec((SMALL_W, GLA_QK), lambda bi, ti: (0, 0)),
            pl.BlockSpec((1, GLA_QK), lambda bi, ti: (0, 0)),
            pl.BlockSpec((1, GLA_DV), lambda bi, ti: (0, 0)),
        ],
        out_specs=pl.BlockSpec((1, tt, GLA_V), lambda bi, ti: (bi, ti, 0)),
        out_shape=jax.ShapeDtypeStruct((b, t, GLA_V), BF16),
        scratch_shapes=[pltpu.VMEM((GLA_HEADS, GLA_DV, GLA_DK), F32)],
        compiler_params=_cparams(("parallel", "arbitrary")),
        name="gla",
    )(plain, plain, small, gates, wgg, bgg, hnw)


def _unit_lower_inverse(a):
    r = lax.broadcasted_iota(jnp.int32, (CHUNK, CHUNK), 0)
    c = lax.broadcasted_iota(jnp.int32, (CHUNK, CHUNK), 1)
    p = -a
    inv = jnp.where(r == c, 1.0, 0.0).astype(F32) + p
    size = 2
    while size < CHUNK:
        p = _dot_hi(p, p)
        inv = inv + _dot_hi(inv, p)
        size *= 2
    return inv


def _gdn_body(q_ref, k_ref, v_ref, sm_ref, smt_ref, gate_ref, cw_ref, par_ref, part_ref, hnw_ref,
              o_ref, s_ref, xe_ref):
    tt = q_ref.shape[1]
    nqkv = 2 * DN_QK + DN_V

    @pl.when(pl.program_id(1) == 0)
    def _():
        s_ref[...] = jnp.zeros_like(s_ref)
        xe_ref[0:CONV_HALO, :] = jnp.zeros((CONV_HALO, nqkv), F32)

    xe_ref[CONV_HALO:CONV_HALO + tt, 0:DN_QK] = q_ref[0].astype(F32)
    xe_ref[CONV_HALO:CONV_HALO + tt, DN_QK:2 * DN_QK] = k_ref[0].astype(F32)
    xe_ref[CONV_HALO:CONV_HALO + tt, 2 * DN_QK:nqkv] = v_ref[0].astype(F32)

    causal, strict, upper = _tri_masks()
    tril = jnp.where(causal, 1.0, 0.0).astype(F32)
    triu = jnp.where(upper, 1.0, 0.0).astype(F32)

    sm = sm_ref[0]
    beta_all = _sigmoid(sm)
    g_all = -jnp.exp(par_ref[0:1, :]) * _softplus(sm + par_ref[1:2, :])
    smt = smt_ref[0]
    gt_all = -jnp.exp(part_ref[:, 0:1]) * _softplus(smt + part_ref[:, 1:2])
    hnw = hnw_ref[...]

    for c in range(tt // CHUNK):
        rows = slice(c * CHUNK, (c + 1) * CHUNK)
        base = CONV_HALO - (CONV_K - 1) + c * CHUNK
        y = cw_ref[0:1, :] * xe_ref[base:base + CHUNK, :]
        for j in range(1, CONV_K):
            y = y + cw_ref[j:j + 1, :] * xe_ref[base + j:base + j + CHUNK, :]
        y = _silu(y)
        g_col_all = _dot_hi(tril, g_all[rows])
        g_row_all = _dot_hi(gt_all[:, rows], triu)
        beta_c = beta_all[rows]
        for h in range(DN_HEADS):
            cols = slice(h * DN_DK, (h + 1) * DN_DK)
            q = y[:, cols]
            k = y[:, DN_QK + h * DN_DK:DN_QK + (h + 1) * DN_DK]
            v = y[:, 2 * DN_QK + h * DN_DV:2 * DN_QK + (h + 1) * DN_DV]
            q = q * lax.rsqrt(jnp.sum(q * q, axis=-1, keepdims=True) + EPS) * (DN_DK ** -0.5)
            k = k * lax.rsqrt(jnp.sum(k * k, axis=-1, keepdims=True) + EPS)
            beta = beta_c[:, LANE_BETA + h:LANE_BETA + h + 1]
            g_col = g_col_all[:, LANE_DA + h:LANE_DA + h + 1]
            g_row = g_row_all[LANE_DA + h:LANE_DA + h + 1, :]
            g_last = g_row[:, CHUNK - 1:CHUNK]
            decay = jnp.exp(jnp.where(causal, g_col - g_row, -jnp.inf))
            exp_g = jnp.exp(g_col)
            k_beta = k * beta
            kb16, k16 = k_beta.astype(BF16), k.astype(BF16)
            kk = jnp.where(strict, _dot_nt(kb16, k16) * decay, 0.0)
            minv = _unit_lower_inverse(kk).astype(BF16)
            u = _dot(minv, (v * beta).astype(BF16))
            w = _dot(minv, (k_beta * exp_g).astype(BF16))
            qk = (_dot_nt(q.astype(BF16), k16) * decay).astype(BF16)
            q_dec = (q * exp_g).astype(BF16)
            k_state = (k * jnp.exp(g_last - g_col)).astype(BF16)
            s = s_ref[h]
            s16 = s.astype(BF16)
            v_new = u - _dot(w.astype(BF16), s16)
            vn16 = v_new.astype(BF16)
            o = _dot(q_dec, s16) + _dot(qk, vn16)
            s_ref[h] = s * jnp.exp(g_last) + _dot_tn(k_state, vn16)
            o = o * lax.rsqrt(jnp.mean(o * o, axis=-1, keepdims=True) + EPS) * hnw
            o_ref[0, rows, cols] = (o * gate_ref[0, rows, cols].astype(F32)).astype(o_ref.dtype)

    xe_ref[0:CONV_HALO, :] = xe_ref[tt:tt + CONV_HALO, :]


def _gdn(plain, small, small_t, gates, conv_w, par, par_t, hnw, b, t):
    tt = min(MIX_TT, t)
    nqkv = 2 * DN_QK + DN_V
    return pl.pallas_call(
        _gdn_body,
        grid=(b, t // tt),
        in_specs=[
            pl.BlockSpec((1, tt, DN_QK), lambda bi, ti: (bi, ti, 2)),
            pl.BlockSpec((1, tt, DN_QK), lambda bi, ti: (bi, ti, 3)),
            pl.BlockSpec((1, tt, DN_V), lambda bi, ti: (bi, ti, 4)),
            pl.BlockSpec((1, tt, SMALL_W), lambda bi, ti: (bi, ti, 0)),
            pl.BlockSpec((1, SMALL_T_ROWS, tt), lambda bi, ti: (bi, 0, ti)),
            pl.BlockSpec((1, tt, DN_V), lambda bi, ti: (bi, ti, 1)),
            pl.BlockSpec((CONV_K, nqkv), lambda bi, ti: (0, 0)),
            pl.BlockSpec((2, SMALL_W), lambda bi, ti: (0, 0)),
            pl.BlockSpec((SMALL_T_ROWS, 2), lambda bi, ti: (0, 0)),
            pl.BlockSpec((1, DN_DV), lambda bi, ti: (0, 0)),
        ],
        out_specs=pl.BlockSpec((1, tt, DN_V), lambda bi, ti: (bi, ti, 0)),
        out_shape=jax.ShapeDtypeStruct((b, t, DN_V), BF16),
        scratch_shapes=[
            pltpu.VMEM((DN_HEADS, DN_DK, DN_DV), F32),
            pltpu.VMEM((CONV_HALO + tt, nqkv), F32),
        ],
        compiler_params=_cparams(("parallel", "arbitrary")),
        name="gdn",
    )(plain, plain, plain, small, small_t, gates, conv_w, par, par_t, hnw)


def _out_body(h_ref, ya_ref, yb_ref, w_ref, o_ref):
    y = (ya_ref[...].astype(F32) + yb_ref[...].astype(F32)).astype(BF16)
    o_ref[...] = h_ref[...] + _dot(y, w_ref[...])


def _out_proj(h2d, ya, yb, w):
    m, d = h2d.shape
    tm = min(OUT_TM, m)
    return pl.pallas_call(
        _out_body,
        grid=(m // tm,),
        in_specs=[
            pl.BlockSpec((tm, d), lambda i: (i, 0)),
            pl.BlockSpec((tm, d), lambda i: (i, 0)),
            pl.BlockSpec((tm, d), lambda i: (i, 0)),
            pl.BlockSpec((d, d), lambda i: (0, 0)),
        ],
        out_specs=pl.BlockSpec((tm, d), lambda i: (i, 0)),
        out_shape=jax.ShapeDtypeStruct((m, d), F32),
        compiler_params=_cparams(("parallel",)),
        name="out_proj",
    )(h2d, ya, yb, w)


def _layer(h, ffn1_norm, ffn1_w_gate, ffn1_w_up, ffn1_w_down, mix_norm, w_in, w_gla_gate, b_gla_gate,
           conv_w, dn_a_log, dn_dt_bias, gla_head_norm, dn_head_norm, w_out, ffn2_norm, ffn2_w_gate,
           ffn2_w_up, ffn2_w_down, final_w, final_norm):
    b, t, d = h.shape
    m = b * t
    row = lambda a: a.reshape(1, -1).astype(F32)

    h1 = _ffn(h.reshape(m, d), row(ffn1_norm), ffn1_w_gate.astype(BF16), ffn1_w_up.astype(BF16),
              ffn1_w_down.astype(BF16), row(final_w), False)

    sizes = (GLA_QK, GLA_QK, GLA_V, GLA_V, GLA_RANK, DN_QK, DN_QK, DN_V, DN_V, DN_HEADS, DN_HEADS,
             D_MODEL, D_MODEL)
    offs = [0]
    for s in sizes:
        offs.append(offs[-1] + s)
    col = lambda i: w_in[:, offs[i]:offs[i + 1]]
    (w_gq, w_gk, w_gv, w_gr, w_glr, w_dq, w_dk, w_dv, w_dgate, w_dbeta, w_da, w_ma, w_mb) = (
        col(i) for i in range(len(sizes)))
    w_plain = jnp.concatenate([w_gq, w_gk, w_gv, w_dq, w_dk, w_dv], axis=1).astype(BF16)
    w_gate_a = jnp.concatenate([w_gr, w_dgate], axis=1).astype(BF16)
    w_gate_b = jnp.concatenate([w_ma, w_mb], axis=1).astype(BF16)
    w_small = jnp.concatenate(
        [w_glr, w_dbeta, w_da, jnp.zeros((d, SMALL_W - LANE_DA - DN_HEADS), F32)], axis=1).astype(BF16)
    w_small_t = w_small[:, :SMALL_T_ROWS].T

    mixn = row(mix_norm)
    plain = _proj_plain(h1, mixn, w_plain).reshape(b, t, -1)
    gates = _proj_gate(h1, mixn, w_gate_a, w_gate_b).reshape(b, t, -1)
    small, small_t = _proj_small(h1.reshape(b, t, d), mixn, w_small, w_small_t)
    small = small.reshape(b, t, SMALL_W)

    wgg = jnp.zeros((SMALL_W, GLA_QK), F32).at[LANE_GLR:LANE_GLR + GLA_RANK].set(w_gla_gate).astype(BF16)
    ya = _gla(plain, small, gates, wgg, row(b_gla_gate), row(gla_head_norm), b, t)

    par = jnp.zeros((2, SMALL_W), F32)
    par = par.at[0, LANE_DA:LANE_DA + DN_HEADS].set(dn_a_log.astype(F32))
    par = par.at[1, LANE_DA:LANE_DA + DN_HEADS].set(dn_dt_bias.astype(F32))
    par_t = par[:, :SMALL_T_ROWS].T
    yb = _gdn(plain, small, small_t, gates, conv_w.astype(F32), par, par_t, row(dn_head_norm), b, t)

    h2 = _out_proj(h1, ya.reshape(m, d), yb.reshape(m, d), w_out.astype(BF16))
    h3 = _ffn(h2, row(ffn2_norm), ffn2_w_gate.astype(BF16), ffn2_w_up.astype(BF16),
              ffn2_w_down.astype(BF16), row(final_w), final_norm)
    return h3.reshape(b, t, d)


def kernel(x, ffn1_norm, ffn1_w_gate, ffn1_w_up, ffn1_w_down, mix_norm, w_in, w_gla_gate, b_gla_gate,
           conv_w, dn_a_log, dn_dt_bias, gla_head_norm, dn_head_norm, w_out, ffn2_norm, ffn2_w_gate,
           ffn2_w_up, ffn2_w_down, final_norm):
    depth = ffn1_norm.shape[0]
    h = x
    for layer in range(depth):
        h = _layer(h, ffn1_norm[layer], ffn1_w_gate[layer], ffn1_w_up[layer], ffn1_w_down[layer],
                   mix_norm[layer], w_in[layer], w_gla_gate[layer], b_gla_gate[layer], conv_w[layer],
                   dn_a_log[layer], dn_dt_bias[layer], gla_head_norm[layer], dn_head_norm[layer],
                   w_out[layer], ffn2_norm[layer], ffn2_w_gate[layer], ffn2_w_up[layer],
                   ffn2_w_down[layer], final_norm, layer == depth - 1)
    return h
```

```python
import functools

import jax
import jax.numpy as jnp
from jax import lax
from jax.experimental import pallas as pl
from jax.experimental.pallas import tpu as pltpu

F32 = jnp.float32
BF16 = jnp.bfloat16

D_MODEL = 1024
D_FF = 2816
FFN_RES = 0.5
GLA_HEADS = 4
GLA_DK = 128
GLA_DV = 256
GLA_RANK = 16
GLA_TAU = 16.0
DN_HEADS = 8
DN_DK = 128
DN_DV = 128
CONV_K = 4
CHUNK = 64
EPS = 1e-6

GLA_QK = GLA_HEADS * GLA_DK
GLA_V = GLA_HEADS * GLA_DV
DN_QK = DN_HEADS * DN_DK
DN_V = DN_HEADS * DN_DV

SMALL_W = 128
SMALL_T_ROWS = 32
LANE_GLR = 0
LANE_BETA = GLA_RANK
LANE_DA = GLA_RANK + DN_HEADS

FFN_TM = 1024
FFN_TF = 256
PROJ_TM = 1024
PROJ_TN = 1024
MIX_TT = 256
OUT_TM = 1024
CONV_HALO = 8

VMEM_LIMIT_BYTES = 48 * 1024 * 1024


def _cparams(sem):
    return pltpu.CompilerParams(dimension_semantics=sem, vmem_limit_bytes=VMEM_LIMIT_BYTES)


def _rms(x, w):
    return x * lax.rsqrt(jnp.mean(x * x, axis=-1, keepdims=True) + EPS) * w


def _sigmoid(x):
    return 1.0 / (1.0 + jnp.exp(-x))


def _silu(x):
    return x * _sigmoid(x)


def _softplus(x):
    return jnp.maximum(x, 0.0) + jnp.log1p(jnp.exp(-jnp.abs(x)))


def _dot(a, b):
    return jnp.dot(a, b, preferred_element_type=F32)


def _dot_nt(a, b):
    return lax.dot_general(a, b, (((1,), (1,)), ((), ())), preferred_element_type=F32)


def _dot_tn(a, b):
    return lax.dot_general(a, b, (((0,), (0,)), ((), ())), preferred_element_type=F32)


def _dot_hi(a, b):
    return jnp.dot(a, b, preferred_element_type=F32, precision=lax.Precision.HIGHEST)


def _ffn_body(x_ref, nw_ref, wg_ref, wu_ref, wd_ref, fw_ref, o_ref, xn_ref, *, final_norm):
    f = pl.program_id(1)

    @pl.when(f == 0)
    def _():
        xn_ref[...] = _rms(x_ref[...], nw_ref[...]).astype(BF16)
        o_ref[...] = jnp.zeros_like(o_ref)

    xn = xn_ref[...]
    g = _dot(xn, wg_ref[...])
    u = _dot(xn, wu_ref[...])
    a = (_silu(g) * u).astype(BF16)
    o_ref[...] += _dot(a, wd_ref[...])

    @pl.when(f == pl.num_programs(1) - 1)
    def _():
        h = x_ref[...] + FFN_RES * o_ref[...]
        if final_norm:
            h = _rms(h, fw_ref[...])
        o_ref[...] = h


def _ffn(x2d, norm_w, wg, wu, wd, final_w, final_norm):
    m, d = x2d.shape
    ff = wg.shape[1]
    tm, tf = min(FFN_TM, m), FFN_TF
    return pl.pallas_call(
        functools.partial(_ffn_body, final_norm=final_norm),
        grid=(m // tm, ff // tf),
        in_specs=[
            pl.BlockSpec((tm, d), lambda i, f: (i, 0)),
            pl.BlockSpec((1, d), lambda i, f: (0, 0)),
            pl.BlockSpec((d, tf), lambda i, f: (0, f)),
            pl.BlockSpec((d, tf), lambda i, f: (0, f)),
            pl.BlockSpec((tf, d), lambda i, f: (f, 0)),
            pl.BlockSpec((1, d), lambda i, f: (0, 0)),
        ],
        out_specs=pl.BlockSpec((tm, d), lambda i, f: (i, 0)),
        out_shape=jax.ShapeDtypeStruct((m, d), F32),
        scratch_shapes=[pltpu.VMEM((tm, d), BF16)],
        compiler_params=_cparams(("parallel", "arbitrary")),
        name="ffn_final" if final_norm else "ffn",
    )(x2d, norm_w, wg, wu, wd, final_w)


def _norm_to_scratch(x_ref, nw_ref, xn_ref):
    @pl.when(pl.program_id(1) == 0)
    def _():
        xn_ref[...] = _rms(x_ref[...], nw_ref[...]).astype(BF16)


def _proj_plain_body(x_ref, nw_ref, w_ref, o_ref, xn_ref):
    _norm_to_scratch(x_ref, nw_ref, xn_ref)
    o_ref[...] = _dot(xn_ref[...], w_ref[...]).astype(o_ref.dtype)


def _proj_gate_body(x_ref, nw_ref, wa_ref, wb_ref, o_ref, xn_ref):
    _norm_to_scratch(x_ref, nw_ref, xn_ref)
    xn = xn_ref[...]
    o_ref[...] = (_silu(_dot(xn, wa_ref[...])) * _sigmoid(_dot(xn, wb_ref[...]))).astype(o_ref.dtype)


def _proj_small_body(x_ref, nw_ref, w_ref, wt_ref, o_ref, ot_ref):
    xn = _rms(x_ref[...], nw_ref[...]).astype(BF16)
    o_ref[...] = _dot(xn, w_ref[...])
    ot_ref[0] = _dot_nt(wt_ref[...], xn)


def _proj_plain(h2d, norm_w, w):
    m, d = h2d.shape
    n = w.shape[1]
    tm, tn = min(PROJ_TM, m), PROJ_TN
    return pl.pallas_call(
        _proj_plain_body,
        grid=(m // tm, n // tn),
        in_specs=[
            pl.BlockSpec((tm, d), lambda i, j: (i, 0)),
            pl.BlockSpec((1, d), lambda i, j: (0, 0)),
            pl.BlockSpec((d, tn), lambda i, j: (0, j)),
        ],
        out_specs=pl.BlockSpec((tm, tn), lambda i, j: (i, j)),
        out_shape=jax.ShapeDtypeStruct((m, n), BF16),
        scratch_shapes=[pltpu.VMEM((tm, d), BF16)],
        compiler_params=_cparams(("parallel", "arbitrary")),
        name="proj_plain",
    )(h2d, norm_w, w)


def _proj_gate(h2d, norm_w, wa, wb):
    m, d = h2d.shape
    n = wa.shape[1]
    tm, tn = min(PROJ_TM, m), PROJ_TN
    return pl.pallas_call(
        _proj_gate_body,
        grid=(m // tm, n // tn),
        in_specs=[
            pl.BlockSpec((tm, d), lambda i, j: (i, 0)),
            pl.BlockSpec((1, d), lambda i, j: (0, 0)),
            pl.BlockSpec((d, tn), lambda i, j: (0, j)),
            pl.BlockSpec((d, tn), lambda i, j: (0, j)),
        ],
        out_specs=pl.BlockSpec((tm, tn), lambda i, j: (i, j)),
        out_shape=jax.ShapeDtypeStruct((m, n), BF16),
        scratch_shapes=[pltpu.VMEM((tm, d), BF16)],
        compiler_params=_cparams(("parallel", "arbitrary")),
        name="proj_gate",
    )(h2d, norm_w, wa, wb)


def _proj_small(h3d, norm_w, w, wt):
    b, t, d = h3d.shape
    tm = min(PROJ_TM, t)
    nt = t // tm
    h2d = h3d.reshape(b * t, d)
    return pl.pallas_call(
        _proj_small_body,
        grid=(b, nt),
        in_specs=[
            pl.BlockSpec((tm, d), lambda bi, ti: (bi * nt + ti, 0)),
            pl.BlockSpec((1, d), lambda bi, ti: (0, 0)),
            pl.BlockSpec((d, SMALL_W), lambda bi, ti: (0, 0)),
            pl.BlockSpec((SMALL_T_ROWS, d), lambda bi, ti: (0, 0)),
        ],
        out_specs=[
            pl.BlockSpec((tm, SMALL_W), lambda bi, ti: (bi * nt + ti, 0)),
            pl.BlockSpec((1, SMALL_T_ROWS, tm), lambda bi, ti: (bi, 0, ti)),
        ],
        out_shape=[
            jax.ShapeDtypeStruct((b * t, SMALL_W), F32),
            jax.ShapeDtypeStruct((b, SMALL_T_ROWS, t), F32),
        ],
        compiler_params=_cparams(("parallel", "parallel")),
        name="proj_small",
    )(h2d, norm_w, w, wt)


def _tri_masks():
    r = lax.broadcasted_iota(jnp.int32, (CHUNK, CHUNK), 0)
    c = lax.broadcasted_iota(jnp.int32, (CHUNK, CHUNK), 1)
    return r >= c, r > c, r <= c


def _gla_body(qk_ref, v_ref, sm_ref, gate_ref, wgg_ref, bgg_ref, hnw_ref, o_ref, st_ref):
    @pl.when(pl.program_id(1) == 0)
    def _():
        st_ref[...] = jnp.zeros_like(st_ref)

    causal, _, _ = _tri_masks()
    tril = jnp.where(causal, 1.0, 0.0).astype(F32)
    tt = qk_ref.shape[1]

    z = _dot(sm_ref[0].astype(BF16), wgg_ref[...]) + bgg_ref[...]
    log_a = (jnp.minimum(z, 0.0) - jnp.log1p(jnp.exp(-jnp.abs(z)))) * (1.0 / GLA_TAU)
    hnw = hnw_ref[...]

    for c in range(tt // CHUNK):
        rows = slice(c * CHUNK, (c + 1) * CHUNK)
        b = _dot_hi(tril, log_a[rows])
        b_last = b[CHUNK - 1:CHUNK, :]
        q = qk_ref[0, rows, 0:GLA_QK].astype(F32) * (GLA_DK ** -0.5)
        k = qk_ref[0, rows, GLA_QK:2 * GLA_QK].astype(F32)
        q_in = (q * jnp.exp(b)).astype(BF16)
        k_neg = (k * jnp.exp(-b)).astype(BF16)
        k_state = (k * jnp.exp(b_last - b)).astype(BF16)
        a_chunk = jnp.exp(b_last)
        for h in range(GLA_HEADS):
            kcols = slice(h * GLA_DK, (h + 1) * GLA_DK)
            vcols = slice(h * GLA_DV, (h + 1) * GLA_DV)
            qh, kh, ksh = q_in[:, kcols], k_neg[:, kcols], k_state[:, kcols]
            vh = v_ref[0, rows, vcols]
            st = st_ref[h]
            scores = jnp.where(causal, _dot_nt(qh, kh), 0.0)
            o = _dot(scores.astype(BF16), vh) + _dot_nt(qh, st.astype(BF16))
            st_ref[h] = st * a_chunk[:, kcols] + _dot_tn(vh, ksh)
            o = o * lax.rsqrt(jnp.mean(o * o, axis=-1, keepdims=True) + EPS) * hnw
            o_ref[0, rows, vcols] = (o * gate_ref[0, rows, vcols].astype(F32)).astype(o_ref.dtype)


def _gla(plain, small, gates, wgg, bgg, hnw, b, t):
    tt = min(MIX_TT, t)
    return pl.pallas_call(
        _gla_body,
        grid=(b, t // tt),
        in_specs=[
            pl.BlockSpec((1, tt, 2 * GLA_QK), lambda bi, ti: (bi, ti, 0)),
            pl.BlockSpec((1, tt, GLA_V), lambda bi, ti: (bi, ti, 1)),
            pl.BlockSpec((1, tt, SMALL_W), lambda bi, ti: (bi, ti, 0)),
            pl.BlockSpec((1, tt, GLA_V), lambda bi, ti: (bi, ti, 0)),
            pl.BlockSpec((SMALL_W, GLA_QK), lambda bi, ti: (0, 0)),
            pl.BlockSpec((1, GLA_QK), lambda bi, ti: (0, 0)),
            pl.BlockSpec((1, GLA_DV), lambda bi, ti: (0, 0)),
        ],
        out_specs=pl.BlockSpec((1, tt, GLA_V), lambda bi, ti: (bi, ti, 0)),
        out_shape=jax.ShapeDtypeStruct((b, t, GLA_V), BF16),
        scratch_shapes=[pltpu.VMEM((GLA_HEADS, GLA_DV, GLA_DK), F32)],
        compiler_params=_cparams(("parallel", "arbitrary")),
        name="gla",
    )(plain, plain, small, gates, wgg, bgg, hnw)


def _unit_lower_inverses(a_list):
    r = lax.broadcasted_iota(jnp.int32, (CHUNK, CHUNK), 0)
    c = lax.broadcasted_iota(jnp.int32, (CHUNK, CHUNK), 1)
    eye = jnp.where(r == c, 1.0, 0.0).astype(F32)
    p = [-a for a in a_list]
    inv = [eye + x for x in p]
    size = 2
    while size < CHUNK:
        p16 = [x.astype(BF16) for x in p]
        p = [_dot(x, x) for x in p16]
        inv = [t + _dot(t.astype(BF16), x.astype(BF16)) for t, x in zip(inv, p)]
        size *= 2
    return inv


def _gdn_body(q_ref, k_ref, v_ref, sm_ref, smt_ref, gate_ref, cw_ref, par_ref, part_ref, hnw_ref,
              o_ref, s_ref, xe_ref, u_ref, wq_ref, qk_ref, ks_ref):
    tt = q_ref.shape[1]
    nc = tt // CHUNK
    nqkv = 2 * DN_QK + DN_V
    heads = range(DN_HEADS)

    @pl.when(pl.program_id(1) == 0)
    def _():
        s_ref[...] = jnp.zeros_like(s_ref)
        xe_ref[0:CONV_HALO, :] = jnp.zeros((CONV_HALO, nqkv), F32)

    xe_ref[CONV_HALO:CONV_HALO + tt, 0:DN_QK] = q_ref[0].astype(F32)
    xe_ref[CONV_HALO:CONV_HALO + tt, DN_QK:2 * DN_QK] = k_ref[0].astype(F32)
    xe_ref[CONV_HALO:CONV_HALO + tt, 2 * DN_QK:nqkv] = v_ref[0].astype(F32)

    causal, strict, _ = _tri_masks()
    r = lax.broadcasted_iota(jnp.int32, (tt, tt), 0)
    c = lax.broadcasted_iota(jnp.int32, (tt, tt), 1)
    same_chunk = (r // CHUNK) == (c // CHUNK)
    bd_tril = jnp.where(same_chunk & (r >= c), 1.0, 0.0).astype(F32)
    bd_triu = jnp.where(same_chunk & (r <= c), 1.0, 0.0).astype(F32)

    sm = sm_ref[0]
    beta_all = _sigmoid(sm)
    g_all = -jnp.exp(par_ref[0:1, :]) * _softplus(sm + par_ref[1:2, :])
    smt = smt_ref[0]
    gt_all = -jnp.exp(part_ref[:, 0:1]) * _softplus(smt + part_ref[:, 1:2])
    g_col_all = _dot_hi(bd_tril, g_all)
    g_row_all = _dot_hi(gt_all, bd_triu)
    hnw = hnw_ref[...]

    for ci in range(nc):
        rows = slice(ci * CHUNK, (ci + 1) * CHUNK)
        base = CONV_HALO - (CONV_K - 1) + ci * CHUNK
        y = cw_ref[0:1, :] * xe_ref[base:base + CHUNK, :]
        for j in range(1, CONV_K):
            y = y + cw_ref[j:j + 1, :] * xe_ref[base + j:base + j + CHUNK, :]
        y = _silu(y)
        qn, kn, vb, kb, decay, exp_g, k_dec = [], [], [], [], [], [], []
        for h in heads:
            q = y[:, h * DN_DK:(h + 1) * DN_DK]
            k = y[:, DN_QK + h * DN_DK:DN_QK + (h + 1) * DN_DK]
            v = y[:, 2 * DN_QK + h * DN_DV:2 * DN_QK + (h + 1) * DN_DV]
            q = q * lax.rsqrt(jnp.sum(q * q, axis=-1, keepdims=True) + EPS) * (DN_DK ** -0.5)
            k = k * lax.rsqrt(jnp.sum(k * k, axis=-1, keepdims=True) + EPS)
            beta = beta_all[rows, LANE_BETA + h:LANE_BETA + h + 1]
            g_col = g_col_all[rows, LANE_DA + h:LANE_DA + h + 1]
            g_row = g_row_all[LANE_DA + h:LANE_DA + h + 1, rows]
            g_last = g_row[:, CHUNK - 1:CHUNK]
            qn.append(q)
            kn.append(k)
            vb.append(v * beta)
            kb.append(k * beta)
            decay.append(jnp.exp(jnp.where(causal, g_col - g_row, -jnp.inf)))
            exp_g.append(jnp.exp(g_col))
            k_dec.append(jnp.exp(g_last - g_col))
        kq = [_dot_nt(jnp.concatenate([kb[h], qn[h]], axis=0).astype(BF16), kn[h].astype(BF16))
              for h in heads]
        minv = _unit_lower_inverses([jnp.where(strict, kq[h][0:CHUNK] * decay[h], 0.0) for h in heads])
        for h in heads:
            rhs = jnp.concatenate([vb[h], kb[h] * exp_g[h]], axis=1).astype(BF16)
            uw = _dot(minv[h].astype(BF16), rhs)
            u_ref[ci, h] = uw[:, 0:DN_DV]
            wq_ref[ci, h, 0:CHUNK, :] = uw[:, DN_DV:].astype(BF16)
            wq_ref[ci, h, CHUNK:, :] = (qn[h] * exp_g[h]).astype(BF16)
            qk_ref[ci, h] = (kq[h][CHUNK:] * decay[h]).astype(BF16)
            ks_ref[ci, h] = (kn[h] * k_dec[h]).astype(BF16)

    for ci in range(nc):
        rows = slice(ci * CHUNK, (ci + 1) * CHUNK)
        last = ci * CHUNK + CHUNK - 1
        s = [s_ref[h] for h in heads]
        ws = [_dot(wq_ref[ci, h], s[h].astype(BF16)) for h in heads]
        vn16 = [(u_ref[ci, h] - ws[h][0:CHUNK]).astype(BF16) for h in heads]
        o = [ws[h][CHUNK:] + _dot(qk_ref[ci, h], vn16[h]) for h in heads]
        for h in heads:
            g_chunk = jnp.exp(g_row_all[LANE_DA + h:LANE_DA + h + 1, last:last + 1])
            s_ref[h] = s[h] * g_chunk + _dot_tn(ks_ref[ci, h], vn16[h])
        for h in heads:
            cols = slice(h * DN_DV, (h + 1) * DN_DV)
            on = o[h] * lax.rsqrt(jnp.mean(o[h] * o[h], axis=-1, keepdims=True) + EPS) * hnw
            o_ref[0, rows, cols] = (on * gate_ref[0, rows, cols].astype(F32)).astype(o_ref.dtype)

    xe_ref[0:CONV_HALO, :] = xe_ref[tt:tt + CONV_HALO, :]


def _gdn(plain, small, small_t, gates, conv_w, par, par_t, hnw, b, t):
    tt = min(MIX_TT, t)
    nqkv = 2 * DN_QK + DN_V
    return pl.pallas_call(
        _gdn_body,
        grid=(b, t // tt),
        in_specs=[
            pl.BlockSpec((1, tt, DN_QK), lambda bi, ti: (bi, ti, 2)),
            pl.BlockSpec((1, tt, DN_QK), lambda bi, ti: (bi, ti, 3)),
            pl.BlockSpec((1, tt, DN_V), lambda bi, ti: (bi, ti, 4)),
            pl.BlockSpec((1, tt, SMALL_W), lambda bi, ti: (bi, ti, 0)),
            pl.BlockSpec((1, SMALL_T_ROWS, tt), lambda bi, ti: (bi, 0, ti)),
            pl.BlockSpec((1, tt, DN_V), lambda bi, ti: (bi, ti, 1)),
            pl.BlockSpec((CONV_K, nqkv), lambda bi, ti: (0, 0)),
            pl.BlockSpec((2, SMALL_W), lambda bi, ti: (0, 0)),
            pl.BlockSpec((SMALL_T_ROWS, 2), lambda bi, ti: (0, 0)),
            pl.BlockSpec((1, DN_DV), lambda bi, ti: (0, 0)),
        ],
        out_specs=pl.BlockSpec((1, tt, DN_V), lambda bi, ti: (bi, ti, 0)),
        out_shape=jax.ShapeDtypeStruct((b, t, DN_V), BF16),
        scratch_shapes=[
            pltpu.VMEM((DN_HEADS, DN_DK, DN_DV), F32),
            pltpu.VMEM((CONV_HALO + tt, nqkv), F32),
            pltpu.VMEM((tt // CHUNK, DN_HEADS, CHUNK, DN_DV), F32),
            pltpu.VMEM((tt // CHUNK, DN_HEADS, 2 * CHUNK, DN_DK), BF16),
            pltpu.VMEM((tt // CHUNK, DN_HEADS, CHUNK, CHUNK), BF16),
            pltpu.VMEM((tt // CHUNK, DN_HEADS, CHUNK, DN_DK), BF16),
        ],
        compiler_params=_cparams(("parallel", "arbitrary")),
        name="gdn",
    )(plain, plain, plain, small, small_t, gates, conv_w, par, par_t, hnw)


def _out_body(h_ref, ya_ref, yb_ref, w_ref, o_ref):
    y = (ya_ref[...].astype(F32) + yb_ref[...].astype(F32)).astype(BF16)
    o_ref[...] = h_ref[...] + _dot(y, w_ref[...])


def _out_proj(h2d, ya, yb, w):
    m, d = h2d.shape
    tm = min(OUT_TM, m)
    return pl.pallas_call(
        _out_body,
        grid=(m // tm,),
        in_specs=[
            pl.BlockSpec((tm, d), lambda i: (i, 0)),
            pl.BlockSpec((tm, d), lambda i: (i, 0)),
            pl.BlockSpec((tm, d), lambda i: (i, 0)),
            pl.BlockSpec((d, d), lambda i: (0, 0)),
        ],
        out_specs=pl.BlockSpec((tm, d), lambda i: (i, 0)),
        out_shape=jax.ShapeDtypeStruct((m, d), F32),
        compiler_params=_cparams(("parallel",)),
        name="out_proj",
    )(h2d, ya, yb, w)


def _layer(h, ffn1_norm, ffn1_w_gate, ffn1_w_up, ffn1_w_down, mix_norm, w_in, w_gla_gate, b_gla_gate,
           conv_w, dn_a_log, dn_dt_bias, gla_head_norm, dn_head_norm, w_out, ffn2_norm, ffn2_w_gate,
           ffn2_w_up, ffn2_w_down, final_w, final_norm):
    b, t, d = h.shape
    m = b * t
    row = lambda a: a.reshape(1, -1).astype(F32)

    h1 = _ffn(h.reshape(m, d), row(ffn1_norm), ffn1_w_gate.astype(BF16), ffn1_w_up.astype(BF16),
              ffn1_w_down.astype(BF16), row(final_w), False)

    sizes = (GLA_QK, GLA_QK, GLA_V, GLA_V, GLA_RANK, DN_QK, DN_QK, DN_V, DN_V, DN_HEADS, DN_HEADS,
             D_MODEL, D_MODEL)
    offs = [0]
    for s in sizes:
        offs.append(offs[-1] + s)
    col = lambda i: w_in[:, offs[i]:offs[i + 1]]
    (w_gq, w_gk, w_gv, w_gr, w_glr, w_dq, w_dk, w_dv, w_dgate, w_dbeta, w_da, w_ma, w_mb) = (
        col(i) for i in range(len(sizes)))
    w_plain = jnp.concatenate([w_gq, w_gk, w_gv, w_dq, w_dk, w_dv], axis=1).astype(BF16)
    w_gate_a = jnp.concatenate([w_gr, w_dgate], axis=1).astype(BF16)
    w_gate_b = jnp.concatenate([w_ma, w_mb], axis=1).astype(BF16)
    w_small = jnp.concatenate(
        [w_glr, w_dbeta, w_da, jnp.zeros((d, SMALL_W - LANE_DA - DN_HEADS), F32)], axis=1).astype(BF16)
    w_small_t = w_small[:, :SMALL_T_ROWS].T

    mixn = row(mix_norm)
    plain = _proj_plain(h1, mixn, w_plain).reshape(b, t, -1)
    gates = _proj_gate(h1, mixn, w_gate_a, w_gate_b).reshape(b, t, -1)
    small, small_t = _proj_small(h1.reshape(b, t, d), mixn, w_small, w_small_t)
    small = small.reshape(b, t, SMALL_W)

    wgg = jnp.zeros((SMALL_W, GLA_QK), F32).at[LANE_GLR:LANE_GLR + GLA_RANK].set(w_gla_gate).astype(BF16)
    ya = _gla(plain, small, gates, wgg, row(b_gla_gate), row(gla_head_norm), b, t)

    par = jnp.zeros((2, SMALL_W), F32)
    par = par.at[0, LANE_DA:LANE_DA + DN_HEADS].set(dn_a_log.astype(F32))
    par = par.at[1, LANE_DA:LANE_DA + DN_HEADS].set(dn_dt_bias.astype(F32))
    par_t = par[:, :SMALL_T_ROWS].T
    yb = _gdn(plain, small, small_t, gates, conv_w.astype(F32), par, par_t, row(dn_head_norm), b, t)

    h2 = _out_proj(h1, ya.reshape(m, d), yb.reshape(m, d), w_out.astype(BF16))
    h3 = _ffn(h2, row(ffn2_norm), ffn2_w_gate.astype(BF16), ffn2_w_up.astype(BF16),
              ffn2_w_down.astype(BF16), row(final_w), final_norm)
    return h3.reshape(b, t, d)


def kernel(x, ffn1_norm, ffn1_w_gate, ffn1_w_up, ffn1_w_down, mix_norm, w_in, w_gla_gate, b_gla_gate,
           conv_w, dn_a_log, dn_dt_bias, gla_head_norm, dn_head_norm, w_out, ffn2_norm, ffn2_w_gate,
           ffn2_w_up, ffn2_w_down, final_norm):
    depth = ffn1_norm.shape[0]
    h = x
    for layer in range(depth):
        h = _layer(h, ffn1_norm[layer], ffn1_w_gate[layer], ffn1_w_up[layer], ffn1_w_down[layer],
                   mix_norm[layer], w_in[layer], w_gla_gate[layer], b_gla_gate[layer], conv_w[layer],
                   dn_a_log[layer], dn_dt_bias[layer], gla_head_norm[layer], dn_head_norm[layer],
                   w_out[layer], ffn2_norm[layer], ffn2_w_gate[layer], ffn2_w_up[layer],
                   ffn2_w_down[layer], final_norm, layer == depth - 1)
    return h
```

```python
import functools

import jax
import jax.numpy as jnp
from jax import lax
from jax.experimental import pallas as pl
from jax.experimental.pallas import tpu as pltpu

F32 = jnp.float32
BF16 = jnp.bfloat16

D_MODEL = 1024
D_FF = 2816
FFN_RES = 0.5
GLA_HEADS = 4
GLA_DK = 128
GLA_DV = 256
GLA_RANK = 16
GLA_TAU = 16.0
DN_HEADS = 8
DN_DK = 128
DN_DV = 128
CONV_K = 4
CHUNK = 64
EPS = 1e-6

GLA_QK = GLA_HEADS * GLA_DK
GLA_V = GLA_HEADS * GLA_DV
DN_QK = DN_HEADS * DN_DK
DN_V = DN_HEADS * DN_DV

SMALL_W = 128
SMALL_T_ROWS = 32
LANE_GLR = 0
LANE_BETA = GLA_RANK
LANE_DA = GLA_RANK + DN_HEADS

FFN_TM = 512
FFN_TF = 256
PROJ_TM = 512
PROJ_TN = 512
CONV_TN = 256
CONV_STRIP = 64
MIX_TT = 256
CONV_HALO = 8

VMEM_LIMIT_BYTES = 48 * 1024 * 1024


def _cparams(sem):
    return pltpu.CompilerParams(dimension_semantics=sem, vmem_limit_bytes=VMEM_LIMIT_BYTES)


def _rms(x, w):
    return x * lax.rsqrt(jnp.mean(x * x, axis=-1, keepdims=True) + EPS) * w


def _sigmoid(x):
    return 1.0 / (1.0 + jnp.exp(-x))


def _silu(x):
    return x * _sigmoid(x)


def _softplus(x):
    return jnp.maximum(x, 0.0) + jnp.log1p(jnp.exp(-jnp.abs(x)))


def _dot(a, b):
    return jnp.dot(a, b, preferred_element_type=F32)


def _dot_nt(a, b):
    return lax.dot_general(a, b, (((1,), (1,)), ((), ())), preferred_element_type=F32)


def _dot_tn(a, b):
    return lax.dot_general(a, b, (((0,), (0,)), ((), ())), preferred_element_type=F32)


def _dot_hi(a, b):
    return jnp.dot(a, b, preferred_element_type=F32, precision=lax.Precision.HIGHEST)


def _swiglu_half_step(x, nw_ref, wg_ref, wu_ref, wd_ref):
    xn = _rms(x, nw_ref[...]).astype(BF16)
    ff = wg_ref.shape[1]
    acc = None
    for f0 in range(0, ff, FFN_TF):
        g = _dot(xn, wg_ref[:, f0:f0 + FFN_TF])
        u = _dot(xn, wu_ref[:, f0:f0 + FFN_TF])
        part = _dot((_silu(g) * u).astype(BF16), wd_ref[f0:f0 + FFN_TF, :])
        acc = part if acc is None else acc + part
    return x + FFN_RES * acc


def _ffn_body(x_ref, nw_ref, wg_ref, wu_ref, wd_ref, o_ref):
    o_ref[...] = _swiglu_half_step(x_ref[...], nw_ref, wg_ref, wu_ref, wd_ref)


def _mix_out_ffn_body(h_ref, ya_ref, yb_ref, wo_ref, nw_ref, wg_ref, wu_ref, wd_ref, fw_ref, o_ref, *,
                      final_norm):
    y = (ya_ref[...].astype(F32) + yb_ref[...].astype(F32)).astype(BF16)
    h = _swiglu_half_step(h_ref[...] + _dot(y, wo_ref[...]), nw_ref, wg_ref, wu_ref, wd_ref)
    if final_norm:
        h = _rms(h, fw_ref[...])
    o_ref[...] = h


def _resident(shape):
    return pl.BlockSpec(shape, lambda *_: (0,) * len(shape), pipeline_mode=pl.Buffered(1))


def _ffn(x2d, norm_w, wg, wu, wd):
    m, d = x2d.shape
    ff = wg.shape[1]
    tm = min(FFN_TM, m)
    row = pl.BlockSpec((tm, d), lambda i: (i, 0))
    return pl.pallas_call(
        _ffn_body,
        grid=(m // tm,),
        in_specs=[row, _resident((1, d)), _resident((d, ff)), _resident((d, ff)), _resident((ff, d))],
        out_specs=row,
        out_shape=jax.ShapeDtypeStruct((m, d), F32),
        compiler_params=_cparams(("parallel",)),
        name="ffn",
    )(x2d, norm_w, wg, wu, wd)


def _mix_out_ffn(h2d, ya, yb, wo, norm_w, wg, wu, wd, final_w, final_norm):
    m, d = h2d.shape
    ff = wg.shape[1]
    tm = min(FFN_TM, m)
    row = pl.BlockSpec((tm, d), lambda i: (i, 0))
    return pl.pallas_call(
        functools.partial(_mix_out_ffn_body, final_norm=final_norm),
        grid=(m // tm,),
        in_specs=[row, row, row, _resident((d, d)), _resident((1, d)), _resident((d, ff)),
                  _resident((d, ff)), _resident((ff, d)), _resident((1, d))],
        out_specs=row,
        out_shape=jax.ShapeDtypeStruct((m, d), F32),
        compiler_params=_cparams(("parallel",)),
        name="mix_out_ffn",
    )(h2d, ya, yb, wo, norm_w, wg, wu, wd, final_w)


def _proj_body(x_ref, nw_ref, wp_ref, wa_ref, wb_ref, ws_ref, wst_ref, cw_ref, plain_ref, gate_ref, sm_ref,
               smt_ref, carry_ref, pe_ref):
    tm = x_ref.shape[0]
    n_gla = 2 * GLA_QK + GLA_V
    n_dn_qk = 2 * DN_QK

    @pl.when(pl.program_id(1) == 0)
    def _():
        carry_ref[...] = jnp.zeros_like(carry_ref)

    xn = _rms(x_ref[...], nw_ref[...]).astype(BF16)

    def plain_chunk(n0):
        cols = slice(n0, n0 + PROJ_TN)
        plain_ref[:, cols] = _dot(xn, wp_ref[:, cols]).astype(plain_ref.dtype)

    def gate_chunk(n0):
        cols = slice(n0, n0 + PROJ_TN)
        gate_ref[:, cols] = (_silu(_dot(xn, wa_ref[:, cols])) *
                             _sigmoid(_dot(xn, wb_ref[:, cols]))).astype(gate_ref.dtype)

    def conv_chunk(ic):
        c0 = ic * CONV_TN
        ccols = slice(c0, c0 + CONV_TN)
        pe = pe_ref.at[ic % 2]
        pe[0:CONV_HALO, :] = carry_ref[:, ccols]
        pe[CONV_HALO:, :] = _dot(xn, wp_ref[:, n_gla + c0:n_gla + c0 + CONV_TN])
        carry_ref[:, ccols] = pe[tm:tm + CONV_HALO, :]
        for r0 in range(0, tm, CONV_STRIP):
            win = pe[r0:r0 + CONV_STRIP + CONV_HALO, :]
            y = cw_ref[CONV_K - 1:CONV_K, ccols] * win[CONV_HALO:, :]
            for j in range(CONV_K - 1):
                off = CONV_HALO - (CONV_K - 1) + j
                tap = pltpu.roll(win, CONV_STRIP + CONV_HALO - off, axis=0)[0:CONV_STRIP, :]
                y = y + cw_ref[j:j + 1, ccols] * tap
            y = _silu(y)
            if c0 < n_dn_qk:
                scale = DN_DK ** -0.5 if c0 < DN_QK else 1.0
                heads = [y[:, h0:h0 + DN_DK] for h0 in range(0, CONV_TN, DN_DK)]
                y = jnp.concatenate(
                    [yh * (lax.rsqrt(jnp.sum(yh * yh, axis=-1, keepdims=True) + EPS) * scale)
                     for yh in heads], axis=1)
            plain_ref[r0:r0 + CONV_STRIP, n_gla + c0:n_gla + c0 + CONV_TN] = y.astype(plain_ref.dtype)

    mxu_tasks = ([functools.partial(plain_chunk, n0) for n0 in range(0, n_gla, PROJ_TN)] +
                 [functools.partial(gate_chunk, n0) for n0 in range(0, wa_ref.shape[1], PROJ_TN)])
    conv_tasks = [functools.partial(conv_chunk, ic) for ic in range((wp_ref.shape[1] - n_gla) // CONV_TN)]
    per_mxu = -(-len(conv_tasks) // len(mxu_tasks))
    for i, task in enumerate(mxu_tasks):
        task()
        for conv_task in conv_tasks[i * per_mxu:(i + 1) * per_mxu]:
            conv_task()
    sm_ref[...] = _dot(xn, ws_ref[...])
    smt_ref[0] = _dot_nt(wst_ref[...], xn)


def _proj(h3d, norm_w, w_plain, w_gate_a, w_gate_b, w_small, w_small_t, conv_w):
    b, t, d = h3d.shape
    tm = min(PROJ_TM, t)
    nt = t // tm
    n_plain, n_gate, n_conv = w_plain.shape[1], w_gate_a.shape[1], conv_w.shape[1]
    rows = lambda n: pl.BlockSpec((tm, n), lambda bi, ti: (bi * nt + ti, 0))
    return pl.pallas_call(
        _proj_body,
        grid=(b, nt),
        in_specs=[rows(d), _resident((1, d)), _resident((d, n_plain)), _resident((d, n_gate)),
                  _resident((d, n_gate)), _resident((d, SMALL_W)), _resident((SMALL_T_ROWS, d)),
                  _resident((CONV_K, n_conv))],
        out_specs=[rows(n_plain), rows(n_gate), rows(SMALL_W),
                   pl.BlockSpec((1, SMALL_T_ROWS, tm), lambda bi, ti: (bi, 0, ti))],
        out_shape=[
            jax.ShapeDtypeStruct((b * t, n_plain), BF16),
            jax.ShapeDtypeStruct((b * t, n_gate), BF16),
            jax.ShapeDtypeStruct((b * t, SMALL_W), F32),
            jax.ShapeDtypeStruct((b, SMALL_T_ROWS, t), F32),
        ],
        scratch_shapes=[pltpu.VMEM((CONV_HALO, n_conv), F32),
                        pltpu.VMEM((2, CONV_HALO + tm, CONV_TN), F32)],
        compiler_params=_cparams(("parallel", "arbitrary")),
        name="proj",
    )(h3d.reshape(b * t, d), norm_w, w_plain, w_gate_a, w_gate_b, w_small, w_small_t, conv_w)


def _tri_masks():
    r = lax.broadcasted_iota(jnp.int32, (CHUNK, CHUNK), 0)
    c = lax.broadcasted_iota(jnp.int32, (CHUNK, CHUNK), 1)
    return r >= c, r > c, r <= c


def _gla_body(qk_ref, v_ref, sm_ref, gate_ref, wgg_ref, bgg_ref, hnw_ref, o_ref, st_ref):
    @pl.when(pl.program_id(1) == 0)
    def _():
        st_ref[...] = jnp.zeros_like(st_ref)

    causal, _, _ = _tri_masks()
    tril = jnp.where(causal, 1.0, 0.0).astype(F32)
    tt = qk_ref.shape[1]

    z = _dot(sm_ref[0].astype(BF16), wgg_ref[...]) + bgg_ref[...]
    log_a = (jnp.minimum(z, 0.0) - jnp.log1p(jnp.exp(-jnp.abs(z)))) * (1.0 / GLA_TAU)
    hnw = hnw_ref[...]

    for c in range(tt // CHUNK):
        rows = slice(c * CHUNK, (c + 1) * CHUNK)
        b = _dot_hi(tril, log_a[rows])
        b_last = b[CHUNK - 1:CHUNK, :]
        q = qk_ref[0, rows, 0:GLA_QK].astype(F32) * (GLA_DK ** -0.5)
        k = qk_ref[0, rows, GLA_QK:2 * GLA_QK].astype(F32)
        q_in = (q * jnp.exp(b)).astype(BF16)
        k_neg = (k * jnp.exp(-b)).astype(BF16)
        k_state = (k * jnp.exp(b_last - b)).astype(BF16)
        a_chunk = jnp.exp(b_last)
        for h in range(GLA_HEADS):
            kcols = slice(h * GLA_DK, (h + 1) * GLA_DK)
            vcols = slice(h * GLA_DV, (h + 1) * GLA_DV)
            qh, kh, ksh = q_in[:, kcols], k_neg[:, kcols], k_state[:, kcols]
            vh = v_ref[0, rows, vcols]
            st = st_ref[h]
            scores = jnp.where(causal, _dot_nt(qh, kh), 0.0)
            o = _dot(scores.astype(BF16), vh) + _dot_nt(qh, st.astype(BF16))
            st_ref[h] = st * a_chunk[:, kcols] + _dot_tn(vh, ksh)
            o = o * lax.rsqrt(jnp.mean(o * o, axis=-1, keepdims=True) + EPS) * hnw
            o_ref[0, rows, vcols] = (o * gate_ref[0, rows, vcols].astype(F32)).astype(o_ref.dtype)


def _gla(plain, small, gates, wgg, bgg, hnw, b, t):
    tt = min(MIX_TT, t)
    return pl.pallas_call(
        _gla_body,
        grid=(b, t // tt),
        in_specs=[
            pl.BlockSpec((1, tt, 2 * GLA_QK), lambda bi, ti: (bi, ti, 0)),
            pl.BlockSpec((1, tt, GLA_V), lambda bi, ti: (bi, ti, 1)),
            pl.BlockSpec((1, tt, SMALL_W), lambda bi, ti: (bi, ti, 0)),
            pl.BlockSpec((1, tt, GLA_V), lambda bi, ti: (bi, ti, 0)),
            pl.BlockSpec((SMALL_W, GLA_QK), lambda bi, ti: (0, 0)),
            pl.BlockSpec((1, GLA_QK), lambda bi, ti: (0, 0)),
            pl.BlockSpec((1, GLA_DV), lambda bi, ti: (0, 0)),
        ],
        out_specs=pl.BlockSpec((1, tt, GLA_V), lambda bi, ti: (bi, ti, 0)),
        out_shape=jax.ShapeDtypeStruct((b, t, GLA_V), BF16),
        scratch_shapes=[pltpu.VMEM((GLA_HEADS, GLA_DV, GLA_DK), F32)],
        compiler_params=_cparams(("parallel", "arbitrary")),
        name="gla",
    )(plain, plain, small, gates, wgg, bgg, hnw)


def _unit_lower_inverses(a_list):
    r = lax.broadcasted_iota(jnp.int32, (CHUNK, CHUNK), 0)
    c = lax.broadcasted_iota(jnp.int32, (CHUNK, CHUNK), 1)
    eye = jnp.where(r == c, 1.0, 0.0).astype(F32)
    p = [-a for a in a_list]
    inv = [eye + x for x in p]
    size = 2
    while size < CHUNK:
        p16 = [x.astype(BF16) for x in p]
        p = [_dot(x, x) for x in p16]
        inv = [t + _dot(t.astype(BF16), x.astype(BF16)) for t, x in zip(inv, p)]
        size *= 2
    return inv


def _gdn_body(q_ref, k_ref, v_ref, sm_ref, smt_ref, gate_ref, par_ref, part_ref, hnw_ref,
              o_ref, s_ref, u_ref, wq_ref, qk_ref, ks_ref):
    tt = q_ref.shape[1]
    nc = tt // CHUNK
    heads = range(DN_HEADS)

    @pl.when(pl.program_id(1) == 0)
    def _():
        s_ref[...] = jnp.zeros_like(s_ref)

    causal, strict, _ = _tri_masks()
    r = lax.broadcasted_iota(jnp.int32, (tt, tt), 0)
    c = lax.broadcasted_iota(jnp.int32, (tt, tt), 1)
    same_chunk = (r // CHUNK) == (c // CHUNK)
    bd_tril = jnp.where(same_chunk & (r >= c), 1.0, 0.0).astype(F32)
    bd_triu = jnp.where(same_chunk & (r <= c), 1.0, 0.0).astype(F32)

    sm = sm_ref[0]
    beta_all = _sigmoid(sm)
    g_all = -jnp.exp(par_ref[0:1, :]) * _softplus(sm + par_ref[1:2, :])
    smt = smt_ref[0]
    gt_all = -jnp.exp(part_ref[:, 0:1]) * _softplus(smt + part_ref[:, 1:2])
    g_col_all = _dot_hi(bd_tril, g_all)
    g_row_all = _dot_hi(gt_all, bd_triu)
    hnw = hnw_ref[...]

    probs = [(ci, h) for ci in range(nc) for h in heads]
    kb, decay, exp_g, kq = [], [], [], []
    for ci, h in probs:
        rows = slice(ci * CHUNK, (ci + 1) * CHUNK)
        q16 = q_ref[0, rows, h * DN_DK:(h + 1) * DN_DK]
        k16 = k_ref[0, rows, h * DN_DK:(h + 1) * DN_DK]
        k = k16.astype(F32)
        beta = beta_all[rows, LANE_BETA + h:LANE_BETA + h + 1]
        g_col = g_col_all[rows, LANE_DA + h:LANE_DA + h + 1]
        g_row = g_row_all[LANE_DA + h:LANE_DA + h + 1, rows]
        g_last = g_row[:, CHUNK - 1:CHUNK]
        kb.append(k * beta)
        decay.append(jnp.exp(jnp.where(causal, g_col - g_row, -jnp.inf)))
        exp_g.append(jnp.exp(g_col))
        wq_ref[ci, h, CHUNK:, :] = (q16.astype(F32) * exp_g[-1]).astype(BF16)
        ks_ref[ci, h] = (k * jnp.exp(g_last - g_col)).astype(BF16)
        kq.append(_dot_nt(jnp.concatenate([kb[-1].astype(BF16), q16], axis=0), k16))
    minv = _unit_lower_inverses([jnp.where(strict, x[0:CHUNK] * d, 0.0) for x, d in zip(kq, decay)])
    for i, (ci, h) in enumerate(probs):
        rows = slice(ci * CHUNK, (ci + 1) * CHUNK)
        beta = beta_all[rows, LANE_BETA + h:LANE_BETA + h + 1]
        vb = v_ref[0, rows, h * DN_DV:(h + 1) * DN_DV].astype(F32) * beta
        rhs = jnp.concatenate([vb, kb[i] * exp_g[i]], axis=1).astype(BF16)
        uw = _dot(minv[i].astype(BF16), rhs)
        u_ref[ci, h] = uw[:, 0:DN_DV]
        wq_ref[ci, h, 0:CHUNK, :] = uw[:, DN_DV:].astype(BF16)
        qk_ref[ci, h] = (kq[i][CHUNK:] * decay[i]).astype(BF16)

    for ci in range(nc):
        rows = slice(ci * CHUNK, (ci + 1) * CHUNK)
        last = ci * CHUNK + CHUNK - 1
        s = [s_ref[h] for h in heads]
        ws = [_dot(wq_ref[ci, h], s[h].astype(BF16)) for h in heads]
        vn16 = [(u_ref[ci, h] - ws[h][0:CHUNK]).astype(BF16) for h in heads]
        o = [ws[h][CHUNK:] + _dot(qk_ref[ci, h], vn16[h]) for h in heads]
        for h in heads:
            g_chunk = jnp.exp(g_row_all[LANE_DA + h:LANE_DA + h + 1, last:last + 1])
            s_ref[h] = s[h] * g_chunk + _dot_tn(ks_ref[ci, h], vn16[h])
        for h in heads:
            cols = slice(h * DN_DV, (h + 1) * DN_DV)
            on = o[h] * lax.rsqrt(jnp.mean(o[h] * o[h], axis=-1, keepdims=True) + EPS) * hnw
            o_ref[0, rows, cols] = (on * gate_ref[0, rows, cols].astype(F32)).astype(o_ref.dtype)


def _gdn(plain, small, small_t, gates, par, par_t, hnw, b, t):
    tt = min(MIX_TT, t)
    return pl.pallas_call(
        _gdn_body,
        grid=(b, t // tt),
        in_specs=[
            pl.BlockSpec((1, tt, DN_QK), lambda bi, ti: (bi, ti, 2)),
            pl.BlockSpec((1, tt, DN_QK), lambda bi, ti: (bi, ti, 3)),
            pl.BlockSpec((1, tt, DN_V), lambda bi, ti: (bi, ti, 4)),
            pl.BlockSpec((1, tt, SMALL_W), lambda bi, ti: (bi, ti, 0)),
            pl.BlockSpec((1, SMALL_T_ROWS, tt), lambda bi, ti: (bi, 0, ti)),
            pl.BlockSpec((1, tt, DN_V), lambda bi, ti: (bi, ti, 1)),
            pl.BlockSpec((2, SMALL_W), lambda bi, ti: (0, 0)),
            pl.BlockSpec((SMALL_T_ROWS, 2), lambda bi, ti: (0, 0)),
            pl.BlockSpec((1, DN_DV), lambda bi, ti: (0, 0)),
        ],
        out_specs=pl.BlockSpec((1, tt, DN_V), lambda bi, ti: (bi, ti, 0)),
        out_shape=jax.ShapeDtypeStruct((b, t, DN_V), BF16),
        scratch_shapes=[
            pltpu.VMEM((DN_HEADS, DN_DK, DN_DV), F32),
            pltpu.VMEM((tt // CHUNK, DN_HEADS, CHUNK, DN_DV), F32),
            pltpu.VMEM((tt // CHUNK, DN_HEADS, 2 * CHUNK, DN_DK), BF16),
            pltpu.VMEM((tt // CHUNK, DN_HEADS, CHUNK, CHUNK), BF16),
            pltpu.VMEM((tt // CHUNK, DN_HEADS, CHUNK, DN_DK), BF16),
        ],
        compiler_params=_cparams(("parallel", "arbitrary")),
        name="gdn",
    )(plain, plain, plain, small, small_t, gates, par, par_t, hnw)


def _layer(h, ffn1_norm, ffn1_w_gate, ffn1_w_up, ffn1_w_down, mix_norm, w_in, w_gla_gate, b_gla_gate,
           conv_w, dn_a_log, dn_dt_bias, gla_head_norm, dn_head_norm, w_out, ffn2_norm, ffn2_w_gate,
           ffn2_w_up, ffn2_w_down, final_w, final_norm):
    b, t, d = h.shape
    m = b * t
    row = lambda a: a.reshape(1, -1).astype(F32)

    h1 = _ffn(h.reshape(m, d), row(ffn1_norm), ffn1_w_gate.astype(BF16), ffn1_w_up.astype(BF16),
              ffn1_w_down.astype(BF16))

    sizes = (GLA_QK, GLA_QK, GLA_V, GLA_V, GLA_RANK, DN_QK, DN_QK, DN_V, DN_V, DN_HEADS, DN_HEADS,
             D_MODEL, D_MODEL)
    offs = [0]
    for s in sizes:
        offs.append(offs[-1] + s)
    col = lambda i: w_in[:, offs[i]:offs[i + 1]]
    (w_gq, w_gk, w_gv, w_gr, w_glr, w_dq, w_dk, w_dv, w_dgate, w_dbeta, w_da, w_ma, w_mb) = (
        col(i) for i in range(len(sizes)))
    w_plain = jnp.concatenate([w_gq, w_gk, w_gv, w_dq, w_dk, w_dv], axis=1).astype(BF16)
    w_gate_a = jnp.concatenate([w_gr, w_dgate], axis=1).astype(BF16)
    w_gate_b = jnp.concatenate([w_ma, w_mb], axis=1).astype(BF16)
    w_small = jnp.concatenate(
        [w_glr, w_dbeta, w_da, jnp.zeros((d, SMALL_W - LANE_DA - DN_HEADS), F32)], axis=1).astype(BF16)
    w_small_t = w_small[:, :SMALL_T_ROWS].T

    mixn = row(mix_norm)
    plain, gates, small, small_t = _proj(h1.reshape(b, t, d), mixn, w_plain, w_gate_a, w_gate_b, w_small,
                                         w_small_t, conv_w.astype(F32))
    plain, gates, small = (a.reshape(b, t, -1) for a in (plain, gates, small))

    wgg = jnp.zeros((SMALL_W, GLA_QK), F32).at[LANE_GLR:LANE_GLR + GLA_RANK].set(w_gla_gate).astype(BF16)
    ya = _gla(plain, small, gates, wgg, row(b_gla_gate), row(gla_head_norm), b, t)

    par = jnp.zeros((2, SMALL_W), F32)
    par = par.at[0, LANE_DA:LANE_DA + DN_HEADS].set(dn_a_log.astype(F32))
    par = par.at[1, LANE_DA:LANE_DA + DN_HEADS].set(dn_dt_bias.astype(F32))
    par_t = par[:, :SMALL_T_ROWS].T
    yb = _gdn(plain, small, small_t, gates, par, par_t, row(dn_head_norm), b, t)

    h3 = _mix_out_ffn(h1, ya.reshape(m, d), yb.reshape(m, d), w_out.astype(BF16), row(ffn2_norm),
                      ffn2_w_gate.astype(BF16), ffn2_w_up.astype(BF16), ffn2_w_down.astype(BF16),
                      row(final_w), final_norm)
    return h3.reshape(b, t, d)


def kernel(x, ffn1_norm, ffn1_w_gate, ffn1_w_up, ffn1_w_down, mix_norm, w_in, w_gla_gate, b_gla_gate,
           conv_w, dn_a_log, dn_dt_bias, gla_head_norm, dn_head_norm, w_out, ffn2_norm, ffn2_w_gate,
           ffn2_w_up, ffn2_w_down, final_norm):
    depth = ffn1_norm.shape[0]
    h = x
    for layer in range(depth):
        h = _layer(h, ffn1_norm[layer], ffn1_w_gate[layer], ffn1_w_up[layer], ffn1_w_down[layer],
                   mix_norm[layer], w_in[layer], w_gla_gate[layer], b_gla_gate[layer], conv_w[layer],
                   dn_a_log[layer], dn_dt_bias[layer], gla_head_norm[layer], dn_head_norm[layer],
                   w_out[layer], ffn2_norm[layer], ffn2_w_gate[layer], ffn2_w_up[layer],
                   ffn2_w_down[layer], final_norm, layer == depth - 1)
    return h
```

```python
import functools

import jax
import jax.numpy as jnp
from jax import lax
from jax.experimental import pallas as pl
from jax.experimental.pallas import tpu as pltpu

F32 = jnp.float32
BF16 = jnp.bfloat16

D_MODEL = 1024
D_FF = 2816
FFN_RES = 0.5
GLA_HEADS = 4
GLA_DK = 128
GLA_DV = 256
GLA_RANK = 16
GLA_TAU = 16.0
DN_HEADS = 8
DN_DK = 128
DN_DV = 128
CONV_K = 4
CHUNK = 64
EPS = 1e-6

GLA_QK = GLA_HEADS * GLA_DK
GLA_V = GLA_HEADS * GLA_DV
DN_QK = DN_HEADS * DN_DK
DN_V = DN_HEADS * DN_DV

SMALL_W = 128
SMALL_T_ROWS = 32
LANE_GLR = 0
LANE_BETA = GLA_RANK
LANE_DA = GLA_RANK + DN_HEADS

FFN_TM = 512
FFN_TF = 256
PROJ_TM = 512
PROJ_TN = 512
CONV_TN = 256
CONV_STRIP = 64
CONV_SLOTS = 4
MIX_TT = 256
GDN_TT = 256
GDN_NB = 1
GDN_GROUP = 4
CONV_HALO = 8

VMEM_LIMIT_BYTES = 48 * 1024 * 1024


def _cparams(sem, flags=None):
    return pltpu.CompilerParams(dimension_semantics=sem, vmem_limit_bytes=VMEM_LIMIT_BYTES, flags=flags)


def _rms(x, w):
    return x * lax.rsqrt(jnp.mean(x * x, axis=-1, keepdims=True) + EPS) * w


def _sigmoid(x):
    return jax.nn.sigmoid(x)


def _silu(x):
    return x * _sigmoid(x)


def _softplus(x):
    return jnp.maximum(x, 0.0) + jnp.log1p(jnp.exp(-jnp.abs(x)))


def _dot(a, b):
    return jnp.dot(a, b, preferred_element_type=F32)


def _dot_nt(a, b):
    return lax.dot_general(a, b, (((1,), (1,)), ((), ())), preferred_element_type=F32)


def _dot_tn(a, b):
    return lax.dot_general(a, b, (((0,), (0,)), ((), ())), preferred_element_type=F32)


def _split3(x):
    hi = x.astype(BF16)
    r1 = x - hi.astype(F32)
    mid = r1.astype(BF16)
    lo = (r1 - mid.astype(F32)).astype(BF16)
    return hi, mid, lo


def _ones_dot_rhs(ones, x):
    n = x.shape[1]
    y = _dot(ones.astype(BF16), jnp.concatenate(_split3(x), axis=1))
    return y[:, 0:n] + y[:, n:2 * n] + y[:, 2 * n:]


def _ones_dot_lhs(x, ones):
    m = x.shape[0]
    y = _dot(jnp.concatenate(_split3(x), axis=0), ones.astype(BF16))
    return y[0:m] + y[m:2 * m] + y[2 * m:]


def _swiglu_half_step(x, nw_ref, wg_ref, wu_ref, wd_ref):
    xn = _rms(x, nw_ref[...]).astype(BF16)
    ff = wg_ref.shape[1]
    acc = None
    for f0 in range(0, ff, FFN_TF):
        g = _dot(xn, wg_ref[:, f0:f0 + FFN_TF])
        u = _dot(xn, wu_ref[:, f0:f0 + FFN_TF])
        part = _dot((_silu(g) * u).astype(BF16), wd_ref[f0:f0 + FFN_TF, :])
        acc = part if acc is None else acc + part
    return x + FFN_RES * acc


def _ffn_body(x_ref, nw_ref, wg_ref, wu_ref, wd_ref, o_ref):
    o_ref[...] = _swiglu_half_step(x_ref[...], nw_ref, wg_ref, wu_ref, wd_ref)


def _mix_out_ffn_body(h_ref, ya_ref, yb_ref, wo_ref, nw_ref, wg_ref, wu_ref, wd_ref, fw_ref, o_ref, *,
                      final_norm):
    y = (ya_ref[...].astype(F32) + yb_ref[...].astype(F32)).astype(BF16)
    h = _swiglu_half_step(h_ref[...] + _dot(y, wo_ref[...]), nw_ref, wg_ref, wu_ref, wd_ref)
    if final_norm:
        h = _rms(h, fw_ref[...])
    o_ref[...] = h


def _resident(shape):
    return pl.BlockSpec(shape, lambda *_: (0,) * len(shape), pipeline_mode=pl.Buffered(1))


def _ffn(x2d, norm_w, wg, wu, wd):
    m, d = x2d.shape
    ff = wg.shape[1]
    tm = min(FFN_TM, m)
    row = pl.BlockSpec((tm, d), lambda i: (i, 0))
    return pl.pallas_call(
        _ffn_body,
        grid=(m // tm,),
        in_specs=[row, _resident((1, d)), _resident((d, ff)), _resident((d, ff)), _resident((ff, d))],
        out_specs=row,
        out_shape=jax.ShapeDtypeStruct((m, d), F32),
        compiler_params=_cparams(("parallel",)),
        name="ffn",
    )(x2d, norm_w, wg, wu, wd)


def _mix_out_ffn(h2d, ya, yb, wo, norm_w, wg, wu, wd, final_w, final_norm):
    m, d = h2d.shape
    ff = wg.shape[1]
    tm = min(FFN_TM, m)
    row = pl.BlockSpec((tm, d), lambda i: (i, 0))
    return pl.pallas_call(
        functools.partial(_mix_out_ffn_body, final_norm=final_norm),
        grid=(m // tm,),
        in_specs=[row, row, row, _resident((d, d)), _resident((1, d)), _resident((d, ff)),
                  _resident((d, ff)), _resident((ff, d)), _resident((1, d))],
        out_specs=row,
        out_shape=jax.ShapeDtypeStruct((m, d), F32),
        compiler_params=_cparams(("parallel",)),
        name="mix_out_ffn",
    )(h2d, ya, yb, wo, norm_w, wg, wu, wd, final_w)


def _proj_body(x_ref, nw_ref, wp_ref, wa_ref, wb_ref, ws_ref, wst_ref, cw_ref, plain_ref, gate_ref, sm_ref,
               smt_ref, carry_ref, pe_ref):
    tm = x_ref.shape[0]
    n_gla = 2 * GLA_QK + GLA_V
    n_dn_qk = 2 * DN_QK

    @pl.when(pl.program_id(1) == 0)
    def _():
        carry_ref[...] = jnp.zeros_like(carry_ref)

    xn = _rms(x_ref[...], nw_ref[...]).astype(BF16)

    def plain_chunk(n0):
        cols = slice(n0, n0 + PROJ_TN)
        plain_ref[:, cols] = _dot(xn, wp_ref[:, cols]).astype(plain_ref.dtype)

    def gate_chunk(n0):
        cols = slice(n0, n0 + PROJ_TN)
        gate_ref[:, cols] = (_silu(_dot(xn, wa_ref[:, cols])) *
                             _sigmoid(_dot(xn, wb_ref[:, cols]))).astype(gate_ref.dtype)

    def conv_chunk(ic):
        c0 = ic * CONV_TN
        ccols = slice(c0, c0 + CONV_TN)
        pe = pe_ref.at[ic % pe_ref.shape[0]]
        pe[0:CONV_HALO, :] = carry_ref[:, ccols]
        pe[CONV_HALO:, :] = _dot(xn, wp_ref[:, n_gla + c0:n_gla + c0 + CONV_TN])
        carry_ref[:, ccols] = pe[tm:tm + CONV_HALO, :]
        for r0 in range(0, tm, CONV_STRIP):
            win = pe[r0:r0 + CONV_STRIP + CONV_HALO, :]
            y = cw_ref[CONV_K - 1:CONV_K, ccols] * win[CONV_HALO:, :]
            for j in range(CONV_K - 1):
                off = CONV_HALO - (CONV_K - 1) + j
                tap = pltpu.roll(win, CONV_STRIP + CONV_HALO - off, axis=0)[0:CONV_STRIP, :]
                y = y + cw_ref[j:j + 1, ccols] * tap
            y = _silu(y)
            if c0 < n_dn_qk:
                scale = DN_DK ** -0.5 if c0 < DN_QK else 1.0
                heads = [y[:, h0:h0 + DN_DK] for h0 in range(0, CONV_TN, DN_DK)]
                y = jnp.concatenate(
                    [yh * (lax.rsqrt(jnp.sum(yh * yh, axis=-1, keepdims=True) + EPS) * scale)
                     for yh in heads], axis=1)
            plain_ref[r0:r0 + CONV_STRIP, n_gla + c0:n_gla + c0 + CONV_TN] = y.astype(plain_ref.dtype)

    mxu_tasks = ([functools.partial(plain_chunk, n0) for n0 in range(0, n_gla, PROJ_TN)] +
                 [functools.partial(gate_chunk, n0) for n0 in range(0, wa_ref.shape[1], PROJ_TN)])
    conv_tasks = [functools.partial(conv_chunk, ic) for ic in range((wp_ref.shape[1] - n_gla) // CONV_TN)]
    for task in conv_tasks + mxu_tasks:
        task()
    sm_ref[...] = _dot(xn, ws_ref[...])
    smt_ref[0] = _dot_nt(wst_ref[...], xn)


def _proj(h3d, norm_w, w_plain, w_gate_a, w_gate_b, w_small, w_small_t, conv_w):
    b, t, d = h3d.shape
    tm = min(PROJ_TM, t)
    nt = t // tm
    n_plain, n_gate, n_conv = w_plain.shape[1], w_gate_a.shape[1], conv_w.shape[1]
    rows = lambda n: pl.BlockSpec((tm, n), lambda bi, ti: (bi * nt + ti, 0))
    return pl.pallas_call(
        _proj_body,
        grid=(b, nt),
        in_specs=[rows(d), _resident((1, d)), _resident((d, n_plain)), _resident((d, n_gate)),
                  _resident((d, n_gate)), _resident((d, SMALL_W)), _resident((SMALL_T_ROWS, d)),
                  _resident((CONV_K, n_conv))],
        out_specs=[rows(n_plain), rows(n_gate), rows(SMALL_W),
                   pl.BlockSpec((1, SMALL_T_ROWS, tm), lambda bi, ti: (bi, 0, ti))],
        out_shape=[
            jax.ShapeDtypeStruct((b * t, n_plain), BF16),
            jax.ShapeDtypeStruct((b * t, n_gate), BF16),
            jax.ShapeDtypeStruct((b * t, SMALL_W), F32),
            jax.ShapeDtypeStruct((b, SMALL_T_ROWS, t), F32),
        ],
        scratch_shapes=[pltpu.VMEM((CONV_HALO, n_conv), F32),
                        pltpu.VMEM((CONV_SLOTS, CONV_HALO + tm, CONV_TN), F32)],
        compiler_params=_cparams(("parallel", "arbitrary")),
        name="proj",
    )(h3d.reshape(b * t, d), norm_w, w_plain, w_gate_a, w_gate_b, w_small, w_small_t, conv_w)


def _tri_masks():
    r = lax.broadcasted_iota(jnp.int32, (CHUNK, CHUNK), 0)
    c = lax.broadcasted_iota(jnp.int32, (CHUNK, CHUNK), 1)
    return r >= c, r > c, r <= c


def _gla_body(qk_ref, v_ref, sm_ref, gate_ref, wgg_ref, bgg_ref, hnw_ref, o_ref, st_ref):
    @pl.when(pl.program_id(1) == 0)
    def _():
        st_ref[...] = jnp.zeros_like(st_ref)

    causal, _, _ = _tri_masks()
    tt = qk_ref.shape[1]
    nc = tt // CHUNK
    r = lax.broadcasted_iota(jnp.int32, (tt, tt), 0)
    c = lax.broadcasted_iota(jnp.int32, (tt, tt), 1)
    bd_tril = jnp.where(((r // CHUNK) == (c // CHUNK)) & (r >= c), 1.0, 0.0).astype(F32)

    z = _dot(sm_ref[0].astype(BF16), wgg_ref[...]) + bgg_ref[...]
    log_a = (jnp.minimum(z, 0.0) - jnp.log1p(jnp.exp(-jnp.abs(z)))) * (1.0 / GLA_TAU)
    b = _ones_dot_rhs(bd_tril, log_a)
    hnw = hnw_ref[...]

    probs = [(ci, h) for ci in range(nc) for h in range(GLA_HEADS)]
    q_in, k_neg, k_state, a_chunk = [], [], [], []
    for ci in range(nc):
        rows = slice(ci * CHUNK, (ci + 1) * CHUNK)
        bc = b[rows]
        b_last = bc[CHUNK - 1:CHUNK, :]
        q = qk_ref[0, rows, 0:GLA_QK].astype(F32) * (GLA_DK ** -0.5)
        k = qk_ref[0, rows, GLA_QK:2 * GLA_QK].astype(F32)
        q_in.append((q * jnp.exp(bc)).astype(BF16))
        k_neg.append((k * jnp.exp(-bc)).astype(BF16))
        k_state.append((k * jnp.exp(b_last - bc)).astype(BF16))
        a_chunk.append(jnp.exp(b_last))
    kcols = lambda h: slice(h * GLA_DK, (h + 1) * GLA_DK)
    vcols = lambda h: slice(h * GLA_DV, (h + 1) * GLA_DV)
    vh = [v_ref[0, ci * CHUNK:(ci + 1) * CHUNK, vcols(h)] for ci, h in probs]
    scores = [jnp.where(causal, _dot_nt(q_in[ci][:, kcols(h)], k_neg[ci][:, kcols(h)]), 0.0).astype(BF16)
              for ci, h in probs]
    o_intra = [_dot(s, v) for s, v in zip(scores, vh)]
    d_st = [_dot_tn(v, k_state[ci][:, kcols(h)]) for v, (ci, h) in zip(vh, probs)]

    for i, (ci, h) in enumerate(probs):
        rows = slice(ci * CHUNK, (ci + 1) * CHUNK)
        st = st_ref[h]
        o = o_intra[i] + _dot_nt(q_in[ci][:, kcols(h)], st.astype(BF16))
        st_ref[h] = st * a_chunk[ci][:, kcols(h)] + d_st[i]
        o = o * lax.rsqrt(jnp.mean(o * o, axis=-1, keepdims=True) + EPS) * hnw
        o_ref[0, rows, vcols(h)] = (o * gate_ref[0, rows, vcols(h)].astype(F32)).astype(o_ref.dtype)


def _gla(plain, small, gates, wgg, bgg, hnw, b, t):
    tt = min(MIX_TT, t)
    return pl.pallas_call(
        _gla_body,
        grid=(b, t // tt),
        in_specs=[
            pl.BlockSpec((1, tt, 2 * GLA_QK), lambda bi, ti: (bi, ti, 0)),
            pl.BlockSpec((1, tt, GLA_V), lambda bi, ti: (bi, ti, 1)),
            pl.BlockSpec((1, tt, SMALL_W), lambda bi, ti: (bi, ti, 0)),
            pl.BlockSpec((1, tt, GLA_V), lambda bi, ti: (bi, ti, 0)),
            pl.BlockSpec((SMALL_W, GLA_QK), lambda bi, ti: (0, 0)),
            pl.BlockSpec((1, GLA_QK), lambda bi, ti: (0, 0)),
            pl.BlockSpec((1, GLA_DV), lambda bi, ti: (0, 0)),
        ],
        out_specs=pl.BlockSpec((1, tt, GLA_V), lambda bi, ti: (bi, ti, 0)),
        out_shape=jax.ShapeDtypeStruct((b, t, GLA_V), BF16),
        scratch_shapes=[pltpu.VMEM((GLA_HEADS, GLA_DV, GLA_DK), F32)],
        compiler_params=_cparams(("parallel", "arbitrary")),
        name="gla",
    )(plain, plain, small, gates, wgg, bgg, hnw)


def _gdn_body(q_ref, k_ref, v_ref, sm_ref, smt_ref, gate_ref, par_ref, part_ref, hnw_ref,
              o_ref, s_ref, u_ref, wq_ref, qk_ref, ks_ref):
    nb, tt = q_ref.shape[0], q_ref.shape[1]
    nc = tt // CHUNK
    heads = range(DN_HEADS)

    @pl.when(pl.program_id(1) == 0)
    def _():
        s_ref[...] = jnp.zeros_like(s_ref)

    causal, strict, _ = _tri_masks()
    r = lax.broadcasted_iota(jnp.int32, (tt, tt), 0)
    c = lax.broadcasted_iota(jnp.int32, (tt, tt), 1)
    same_chunk = (r // CHUNK) == (c // CHUNK)
    bd_tril = jnp.where(same_chunk & (r >= c), 1.0, 0.0).astype(F32)
    bd_triu = jnp.where(same_chunk & (r <= c), 1.0, 0.0).astype(F32)

    beta_all, g_col_all, g_row_all = [], [], []
    for bi in range(nb):
        sm = sm_ref[bi]
        beta_all.append(_sigmoid(sm))
        g_all = -jnp.exp(par_ref[0:1, :]) * _softplus(sm + par_ref[1:2, :])
        smt = smt_ref[bi]
        gt_all = -jnp.exp(part_ref[:, 0:1]) * _softplus(smt + part_ref[:, 1:2])
        g_col_all.append(_ones_dot_rhs(bd_tril, g_all))
        g_row_all.append(_ones_dot_lhs(gt_all, bd_triu))
    hnw = hnw_ref[...]

    ri = lax.broadcasted_iota(jnp.int32, (CHUNK, CHUNK), 0)
    ci_ = lax.broadcasted_iota(jnp.int32, (CHUNK, CHUNK), 1)
    eye = jnp.where(ri == ci_, 1.0, 0.0).astype(F32)

    def phase_a(chunks):
        probs = [(bi, ci, h) for bi in range(nb) for ci in chunks for h in heads]
        kb, decay, exp_g, kq = [], [], [], []
        for bi, ci, h in probs:
            rows = slice(ci * CHUNK, (ci + 1) * CHUNK)
            q16 = q_ref[bi, rows, h * DN_DK:(h + 1) * DN_DK]
            k16 = k_ref[bi, rows, h * DN_DK:(h + 1) * DN_DK]
            k = k16.astype(F32)
            beta = beta_all[bi][rows, LANE_BETA + h:LANE_BETA + h + 1]
            g_col = g_col_all[bi][rows, LANE_DA + h:LANE_DA + h + 1]
            g_row = g_row_all[bi][LANE_DA + h:LANE_DA + h + 1, rows]
            g_last = g_row[:, CHUNK - 1:CHUNK]
            kb.append(k * beta)
            decay.append(jnp.exp(jnp.where(causal, g_col - g_row, -jnp.inf)))
            exp_g.append(jnp.exp(g_col))
            wq_ref[bi, ci, h, CHUNK:, :] = (q16.astype(F32) * exp_g[-1]).astype(BF16)
            ks_ref[bi, ci, h] = (k * jnp.exp(g_last - g_col)).astype(BF16)
            kq.append(_dot_nt(jnp.concatenate([kb[-1].astype(BF16), q16], axis=0), k16))
        yield
        p = [-jnp.where(strict, x[0:CHUNK] * d, 0.0) for x, d in zip(kq, decay)]
        inv = [eye + x for x in p]
        size = 2
        while size < CHUNK:
            p16 = [x.astype(BF16) for x in p]
            p = [_dot(x, x) for x in p16]
            yield
            inv = [t + _dot(t.astype(BF16), x.astype(BF16)) for t, x in zip(inv, p)]
            yield
            size *= 2
        for i, (bi, ci, h) in enumerate(probs):
            rows = slice(ci * CHUNK, (ci + 1) * CHUNK)
            beta = beta_all[bi][rows, LANE_BETA + h:LANE_BETA + h + 1]
            vb = v_ref[bi, rows, h * DN_DV:(h + 1) * DN_DV].astype(F32) * beta
            rhs = jnp.concatenate([vb, kb[i] * exp_g[i]], axis=1).astype(BF16)
            uw = _dot(inv[i].astype(BF16), rhs)
            u_ref[bi, ci, h] = uw[:, 0:DN_DV]
            wq_ref[bi, ci, h, 0:CHUNK, :] = uw[:, DN_DV:].astype(BF16)
            qk_ref[bi, ci, h] = (kq[i][CHUNK:] * decay[i]).astype(BF16)
        yield

    def phase_b(chunks):
        seqs = [(bi, h) for bi in range(nb) for h in heads]
        for ci in chunks:
            rows = slice(ci * CHUNK, (ci + 1) * CHUNK)
            last = ci * CHUNK + CHUNK - 1
            s = [s_ref[bi, h] for bi, h in seqs]
            ws = [_dot(wq_ref[bi, ci, h], x.astype(BF16)) for (bi, h), x in zip(seqs, s)]
            yield
            vn16 = [(u_ref[bi, ci, h] - x[0:CHUNK]).astype(BF16) for (bi, h), x in zip(seqs, ws)]
            o = [x[CHUNK:] + _dot(qk_ref[bi, ci, h], vn) for (bi, h), x, vn in zip(seqs, ws, vn16)]
            for (bi, h), x, vn in zip(seqs, s, vn16):
                g_chunk = jnp.exp(g_row_all[bi][LANE_DA + h:LANE_DA + h + 1, last:last + 1])
                s_ref[bi, h] = x * g_chunk + _dot_tn(ks_ref[bi, ci, h], vn)
            yield
            for (bi, h), x in zip(seqs, o):
                cols = slice(h * DN_DV, (h + 1) * DN_DV)
                on = x * lax.rsqrt(jnp.mean(x * x, axis=-1, keepdims=True) + EPS) * hnw
                o_ref[bi, rows, cols] = (on * gate_ref[bi, rows, cols].astype(F32)).astype(o_ref.dtype)
            yield

    def interleave(*streams):
        streams = list(streams)
        while streams:
            for g in list(streams):
                if next(g, StopIteration) is StopIteration:
                    streams.remove(g)

    groups = [list(range(c0, min(c0 + GDN_GROUP, nc))) for c0 in range(0, nc, GDN_GROUP)]
    interleave(phase_a(groups[0]))
    for prev, cur in zip(groups[:-1], groups[1:]):
        interleave(phase_a(cur), phase_b(prev))
    interleave(phase_b(groups[-1]))


def _gdn(plain, small, small_t, gates, par, par_t, hnw, b, t):
    tt = min(GDN_TT, t)
    nb = GDN_NB if b % GDN_NB == 0 else 1
    nc = tt // CHUNK
    return pl.pallas_call(
        _gdn_body,
        grid=(b // nb, t // tt),
        in_specs=[
            pl.BlockSpec((nb, tt, DN_QK), lambda bi, ti: (bi, ti, 2)),
            pl.BlockSpec((nb, tt, DN_QK), lambda bi, ti: (bi, ti, 3)),
            pl.BlockSpec((nb, tt, DN_V), lambda bi, ti: (bi, ti, 4)),
            pl.BlockSpec((nb, tt, SMALL_W), lambda bi, ti: (bi, ti, 0)),
            pl.BlockSpec((nb, SMALL_T_ROWS, tt), lambda bi, ti: (bi, 0, ti)),
            pl.BlockSpec((nb, tt, DN_V), lambda bi, ti: (bi, ti, 1)),
            pl.BlockSpec((2, SMALL_W), lambda bi, ti: (0, 0)),
            pl.BlockSpec((SMALL_T_ROWS, 2), lambda bi, ti: (0, 0)),
            pl.BlockSpec((1, DN_DV), lambda bi, ti: (0, 0)),
        ],
        out_specs=pl.BlockSpec((nb, tt, DN_V), lambda bi, ti: (bi, ti, 0)),
        out_shape=jax.ShapeDtypeStruct((b, t, DN_V), BF16),
        scratch_shapes=[
            pltpu.VMEM((nb, DN_HEADS, DN_DK, DN_DV), F32),
            pltpu.VMEM((nb, nc, DN_HEADS, CHUNK, DN_DV), F32),
            pltpu.VMEM((nb, nc, DN_HEADS, 2 * CHUNK, DN_DK), BF16),
            pltpu.VMEM((nb, nc, DN_HEADS, CHUNK, CHUNK), BF16),
            pltpu.VMEM((nb, nc, DN_HEADS, CHUNK, DN_DK), BF16),
        ],
        compiler_params=_cparams(("parallel", "arbitrary")),
        name="gdn",
    )(plain, plain, plain, small, small_t, gates, par, par_t, hnw)


def _layer(h, ffn1_norm, ffn1_w_gate, ffn1_w_up, ffn1_w_down, mix_norm, w_in, w_gla_gate, b_gla_gate,
           conv_w, dn_a_log, dn_dt_bias, gla_head_norm, dn_head_norm, w_out, ffn2_norm, ffn2_w_gate,
           ffn2_w_up, ffn2_w_down, final_w, final_norm):
    b, t, d = h.shape
    m = b * t
    row = lambda a: a.reshape(1, -1).astype(F32)

    h1 = _ffn(h.reshape(m, d), row(ffn1_norm), ffn1_w_gate.astype(BF16), ffn1_w_up.astype(BF16),
              ffn1_w_down.astype(BF16))

    sizes = (GLA_QK, GLA_QK, GLA_V, GLA_V, GLA_RANK, DN_QK, DN_QK, DN_V, DN_V, DN_HEADS, DN_HEADS,
             D_MODEL, D_MODEL)
    offs = [0]
    for s in sizes:
        offs.append(offs[-1] + s)
    col = lambda i: w_in[:, offs[i]:offs[i + 1]]
    (w_gq, w_gk, w_gv, w_gr, w_glr, w_dq, w_dk, w_dv, w_dgate, w_dbeta, w_da, w_ma, w_mb) = (
        col(i) for i in range(len(sizes)))
    w_plain = jnp.concatenate([w_gq, w_gk, w_gv, w_dq, w_dk, w_dv], axis=1).astype(BF16)
    w_gate_a = jnp.concatenate([w_gr, w_dgate], axis=1).astype(BF16)
    w_gate_b = jnp.concatenate([w_ma, w_mb], axis=1).astype(BF16)
    w_small = jnp.concatenate(
        [w_glr, w_dbeta, w_da, jnp.zeros((d, SMALL_W - LANE_DA - DN_HEADS), F32)], axis=1).astype(BF16)
    w_small_t = w_small[:, :SMALL_T_ROWS].T

    mixn = row(mix_norm)
    plain, gates, small, small_t = _proj(h1.reshape(b, t, d), mixn, w_plain, w_gate_a, w_gate_b, w_small,
                                         w_small_t, conv_w.astype(F32))
    plain, gates, small = (a.reshape(b, t, -1) for a in (plain, gates, small))

    wgg = jnp.zeros((SMALL_W, GLA_QK), F32).at[LANE_GLR:LANE_GLR + GLA_RANK].set(w_gla_gate).astype(BF16)
    ya = _gla(plain, small, gates, wgg, row(b_gla_gate), row(gla_head_norm), b, t)

    par = jnp.zeros((2, SMALL_W), F32)
    par = par.at[0, LANE_DA:LANE_DA + DN_HEADS].set(dn_a_log.astype(F32))
    par = par.at[1, LANE_DA:LANE_DA + DN_HEADS].set(dn_dt_bias.astype(F32))
    par_t = par[:, :SMALL_T_ROWS].T
    yb = _gdn(plain, small, small_t, gates, par, par_t, row(dn_head_norm), b, t)

    h3 = _mix_out_ffn(h1, ya.reshape(m, d), yb.reshape(m, d), w_out.astype(BF16), row(ffn2_norm),
                      ffn2_w_gate.astype(BF16), ffn2_w_up.astype(BF16), ffn2_w_down.astype(BF16),
                      row(final_w), final_norm)
    return h3.reshape(b, t, d)


def kernel(x, ffn1_norm, ffn1_w_gate, ffn1_w_up, ffn1_w_down, mix_norm, w_in, w_gla_gate, b_gla_gate,
           conv_w, dn_a_log, dn_dt_bias, gla_head_norm, dn_head_norm, w_out, ffn2_norm, ffn2_w_gate,
           ffn2_w_up, ffn2_w_down, final_norm):
    depth = ffn1_norm.shape[0]
    h = x
    for layer in range(depth):
        h = _layer(h, ffn1_norm[layer], ffn1_w_gate[layer], ffn1_w_up[layer], ffn1_w_down[layer],
                   mix_norm[layer], w_in[layer], w_gla_gate[layer], b_gla_gate[layer], conv_w[layer],
                   dn_a_log[layer], dn_dt_bias[layer], gla_head_norm[layer], dn_head_norm[layer],
                   w_out[layer], ffn2_norm[layer], ffn2_w_gate[layer], ffn2_w_up[layer],
                   ffn2_w_down[layer], final_norm, layer == depth - 1)
    return h
```

```python
import functools
import itertools

import jax
import jax.numpy as jnp
from jax import lax
from jax.experimental import pallas as pl
from jax.experimental.pallas import tpu as pltpu

F32 = jnp.float32
BF16 = jnp.bfloat16

D_MODEL = 1024
D_FF = 2816
FFN_RES = 0.5
GLA_HEADS = 4
GLA_DK = 128
GLA_DV = 256
GLA_RANK = 16
GLA_TAU = 16.0
DN_HEADS = 8
DN_DK = 128
DN_DV = 128
CONV_K = 4
CHUNK = 64
EPS = 1e-6

GLA_QK = GLA_HEADS * GLA_DK
GLA_V = GLA_HEADS * GLA_DV
DN_QK = DN_HEADS * DN_DK
DN_V = DN_HEADS * DN_DV

SMALL_W = 128
SMALL_T_ROWS = 32
LANE_GLR = 0
LANE_BETA = GLA_RANK
LANE_DA = GLA_RANK + DN_HEADS

FFN_TM = 512
FFN_TF = 256
PROJ_TM = 512
PROJ_TN = 256
CONV_TN = 256
CONV_STRIP = 64
CONV_PIECE = 2
CONV_SLOTS = 4
MIX_TT = 256
GDN_TT = 256
GDN_NB = 1
GDN_YIELD = 8
CONV_HALO = 8

VMEM_LIMIT_BYTES = 48 * 1024 * 1024


def _cparams(sem, flags=None):
    return pltpu.CompilerParams(dimension_semantics=sem, vmem_limit_bytes=VMEM_LIMIT_BYTES, flags=flags)


def _rms(x, w):
    return x * lax.rsqrt(jnp.mean(x * x, axis=-1, keepdims=True) + EPS) * w


def _sigmoid(x):
    return jax.nn.sigmoid(x)


def _silu(x):
    return x * _sigmoid(x)


def _softplus(x):
    return jnp.maximum(x, 0.0) + jnp.log1p(jnp.exp(-jnp.abs(x)))


def _dot(a, b):
    return jnp.dot(a, b, preferred_element_type=F32)


def _dot_nt(a, b):
    return lax.dot_general(a, b, (((1,), (1,)), ((), ())), preferred_element_type=F32)


def _dot_tn(a, b):
    return lax.dot_general(a, b, (((0,), (0,)), ((), ())), preferred_element_type=F32)


def _split3(x):
    hi = x.astype(BF16)
    r1 = x - hi.astype(F32)
    mid = r1.astype(BF16)
    lo = (r1 - mid.astype(F32)).astype(BF16)
    return hi, mid, lo


def _ones_dot_rhs(ones, x):
    n = x.shape[1]
    y = _dot(ones.astype(BF16), jnp.concatenate(_split3(x), axis=1))
    return y[:, 0:n] + y[:, n:2 * n] + y[:, 2 * n:]


def _ones_dot_lhs(x, ones):
    m = x.shape[0]
    y = _dot(jnp.concatenate(_split3(x), axis=0), ones.astype(BF16))
    return y[0:m] + y[m:2 * m] + y[2 * m:]


def _swiglu_half_step(x, nw_ref, wg_ref, wu_ref, wd_ref):
    xn = _rms(x, nw_ref[...]).astype(BF16)
    ff = wg_ref.shape[1]
    acc = None
    for f0 in range(0, ff, FFN_TF):
        g = _dot(xn, wg_ref[:, f0:f0 + FFN_TF])
        u = _dot(xn, wu_ref[:, f0:f0 + FFN_TF])
        part = _dot((_silu(g) * u).astype(BF16), wd_ref[f0:f0 + FFN_TF, :])
        acc = part if acc is None else acc + part
    return x + FFN_RES * acc


def _ffn_body(x_ref, nw_ref, wg_ref, wu_ref, wd_ref, o_ref):
    o_ref[...] = _swiglu_half_step(x_ref[...], nw_ref, wg_ref, wu_ref, wd_ref)


def _mix_out_ffn_body(h_ref, ya_ref, yb_ref, wo_ref, nw_ref, wg_ref, wu_ref, wd_ref, fw_ref, o_ref, *,
                      final_norm):
    y = (ya_ref[...].astype(F32) + yb_ref[...].astype(F32)).astype(BF16)
    h = _swiglu_half_step(h_ref[...] + _dot(y, wo_ref[...]), nw_ref, wg_ref, wu_ref, wd_ref)
    if final_norm:
        h = _rms(h, fw_ref[...])
    o_ref[...] = h


def _resident(shape):
    return pl.BlockSpec(shape, lambda *_: (0,) * len(shape), pipeline_mode=pl.Buffered(1))


def _ffn(x2d, norm_w, wg, wu, wd):
    m, d = x2d.shape
    ff = wg.shape[1]
    tm = min(FFN_TM, m)
    row = pl.BlockSpec((tm, d), lambda i: (i, 0))
    return pl.pallas_call(
        _ffn_body,
        grid=(m // tm,),
        in_specs=[row, _resident((1, d)), _resident((d, ff)), _resident((d, ff)), _resident((ff, d))],
        out_specs=row,
        out_shape=jax.ShapeDtypeStruct((m, d), F32),
        compiler_params=_cparams(("parallel",)),
        name="ffn",
    )(x2d, norm_w, wg, wu, wd)


def _mix_out_ffn(h2d, ya, yb, wo, norm_w, wg, wu, wd, final_w, final_norm):
    m, d = h2d.shape
    ff = wg.shape[1]
    tm = min(FFN_TM, m)
    row = pl.BlockSpec((tm, d), lambda i: (i, 0))
    return pl.pallas_call(
        functools.partial(_mix_out_ffn_body, final_norm=final_norm),
        grid=(m // tm,),
        in_specs=[row, row, row, _resident((d, d)), _resident((1, d)), _resident((d, ff)),
                  _resident((d, ff)), _resident((ff, d)), _resident((1, d))],
        out_specs=row,
        out_shape=jax.ShapeDtypeStruct((m, d), F32),
        compiler_params=_cparams(("parallel",)),
        name="mix_out_ffn",
    )(h2d, ya, yb, wo, norm_w, wg, wu, wd, final_w)


def _proj_body(x_ref, nw_ref, wp_ref, wa_ref, wb_ref, ws_ref, wst_ref, cw_ref, plain_ref, gate_ref, sm_ref,
               smt_ref, carry_ref, pe_ref):
    tm = x_ref.shape[0]
    n_gla = 2 * GLA_QK + GLA_V
    n_dn_qk = 2 * DN_QK

    @pl.when(pl.program_id(1) == 0)
    def _():
        carry_ref[...] = jnp.zeros_like(carry_ref)

    xn = _rms(x_ref[...], nw_ref[...]).astype(BF16)

    def plain_chunk(n0):
        cols = slice(n0, n0 + PROJ_TN)
        plain_ref[:, cols] = _dot(xn, wp_ref[:, cols]).astype(plain_ref.dtype)

    def gate_chunk(n0):
        cols = slice(n0, n0 + PROJ_TN)
        gate_ref[:, cols] = (_silu(_dot(xn, wa_ref[:, cols])) *
                             _sigmoid(_dot(xn, wb_ref[:, cols]))).astype(gate_ref.dtype)

    def conv_dot(ic):
        c0 = ic * CONV_TN
        ccols = slice(c0, c0 + CONV_TN)
        pe = pe_ref.at[ic % pe_ref.shape[0]]
        pe[0:CONV_HALO, :] = carry_ref[:, ccols]
        pe[CONV_HALO:, :] = _dot(xn, wp_ref[:, n_gla + c0:n_gla + c0 + CONV_TN])
        carry_ref[:, ccols] = pe[tm:tm + CONV_HALO, :]

    def conv_strips(ic):
        c0 = ic * CONV_TN
        ccols = slice(c0, c0 + CONV_TN)
        pe = pe_ref.at[ic % pe_ref.shape[0]]
        for r0 in range(0, tm, CONV_STRIP):
            win = pe[r0:r0 + CONV_STRIP + CONV_HALO, :]
            y = cw_ref[CONV_K - 1:CONV_K, ccols] * win[CONV_HALO:, :]
            for j in range(CONV_K - 1):
                off = CONV_HALO - (CONV_K - 1) + j
                tap = pltpu.roll(win, CONV_STRIP + CONV_HALO - off, axis=0)[0:CONV_STRIP, :]
                y = y + cw_ref[j:j + 1, ccols] * tap
            y = _silu(y)
            if c0 < n_dn_qk:
                scale = DN_DK ** -0.5 if c0 < DN_QK else 1.0
                heads = [y[:, h0:h0 + DN_DK] for h0 in range(0, CONV_TN, DN_DK)]
                y = jnp.concatenate(
                    [yh * (lax.rsqrt(jnp.sum(yh * yh, axis=-1, keepdims=True) + EPS) * scale)
                     for yh in heads], axis=1)
            plain_ref[r0:r0 + CONV_STRIP, n_gla + c0:n_gla + c0 + CONV_TN] = y.astype(plain_ref.dtype)
            if (r0 // CONV_STRIP) % CONV_PIECE == CONV_PIECE - 1:
                yield

    def mxu_pieces(tasks):
        for task in tasks:
            task()
            yield

    plain_tasks = [functools.partial(plain_chunk, n0) for n0 in range(0, n_gla, PROJ_TN)]
    gate_tasks = [functools.partial(gate_chunk, n0) for n0 in range(0, wa_ref.shape[1], PROJ_TN)]
    others = [t for pair in itertools.zip_longest(plain_tasks, gate_tasks) for t in pair if t is not None]
    n_conv = (wp_ref.shape[1] - n_gla) // CONV_TN
    conv_dot(0)
    for ic in range(n_conv):
        lo, hi = ic * len(others) // n_conv, (ic + 1) * len(others) // n_conv
        if ic + 1 < n_conv:
            conv_dot(ic + 1)
        _interleave(conv_strips(ic), mxu_pieces(others[lo:hi]))
    sm_ref[...] = _dot(xn, ws_ref[...])
    smt_ref[0] = _dot_nt(wst_ref[...], xn)


def _proj(h3d, norm_w, w_plain, w_gate_a, w_gate_b, w_small, w_small_t, conv_w):
    b, t, d = h3d.shape
    tm = min(PROJ_TM, t)
    nt = t // tm
    n_plain, n_gate, n_conv = w_plain.shape[1], w_gate_a.shape[1], conv_w.shape[1]
    rows = lambda n: pl.BlockSpec((tm, n), lambda bi, ti: (bi * nt + ti, 0))
    return pl.pallas_call(
        _proj_body,
        grid=(b, nt),
        in_specs=[rows(d), _resident((1, d)), _resident((d, n_plain)), _resident((d, n_gate)),
                  _resident((d, n_gate)), _resident((d, SMALL_W)), _resident((SMALL_T_ROWS, d)),
                  _resident((CONV_K, n_conv))],
        out_specs=[rows(n_plain), rows(n_gate), rows(SMALL_W),
                   pl.BlockSpec((1, SMALL_T_ROWS, tm), lambda bi, ti: (bi, 0, ti))],
        out_shape=[
            jax.ShapeDtypeStruct((b * t, n_plain), BF16),
            jax.ShapeDtypeStruct((b * t, n_gate), BF16),
            jax.ShapeDtypeStruct((b * t, SMALL_W), F32),
            jax.ShapeDtypeStruct((b, SMALL_T_ROWS, t), F32),
        ],
        scratch_shapes=[pltpu.VMEM((CONV_HALO, n_conv), F32),
                        pltpu.VMEM((CONV_SLOTS, CONV_HALO + tm, CONV_TN), F32)],
        compiler_params=_cparams(("parallel", "arbitrary")),
        name="proj",
    )(h3d.reshape(b * t, d), norm_w, w_plain, w_gate_a, w_gate_b, w_small, w_small_t, conv_w)


def _tri_masks():
    r = lax.broadcasted_iota(jnp.int32, (CHUNK, CHUNK), 0)
    c = lax.broadcasted_iota(jnp.int32, (CHUNK, CHUNK), 1)
    return r >= c, r > c, r <= c


def _gla_stream(qk_ref, v_ref, sm_ref, gate_ref, wgg_ref, bgg_ref, hnw_ref, o_ref, st_ref):
    causal, _, _ = _tri_masks()
    tt = qk_ref.shape[1]
    nc = tt // CHUNK
    r = lax.broadcasted_iota(jnp.int32, (tt, tt), 0)
    c = lax.broadcasted_iota(jnp.int32, (tt, tt), 1)
    bd_tril = jnp.where(((r // CHUNK) == (c // CHUNK)) & (r >= c), 1.0, 0.0).astype(F32)

    log_a = []
    for ci in range(nc):
        z = _dot(sm_ref[0, ci * CHUNK:(ci + 1) * CHUNK, :].astype(BF16), wgg_ref[...]) + bgg_ref[...]
        log_a.append((jnp.minimum(z, 0.0) - jnp.log1p(jnp.exp(-jnp.abs(z)))) * (1.0 / GLA_TAU))
        yield
    b = _ones_dot_rhs(bd_tril, jnp.concatenate(log_a, axis=0))
    hnw = hnw_ref[...]
    yield

    probs = [(ci, h) for ci in range(nc) for h in range(GLA_HEADS)]
    q_in, k_neg, k_state, a_chunk = [], [], [], []
    for ci in range(nc):
        rows = slice(ci * CHUNK, (ci + 1) * CHUNK)
        bc = b[rows]
        b_last = bc[CHUNK - 1:CHUNK, :]
        q = qk_ref[0, rows, 0:GLA_QK].astype(F32) * (GLA_DK ** -0.5)
        k = qk_ref[0, rows, GLA_QK:2 * GLA_QK].astype(F32)
        q_in.append((q * jnp.exp(bc)).astype(BF16))
        k_neg.append((k * jnp.exp(-bc)).astype(BF16))
        k_state.append((k * jnp.exp(b_last - bc)).astype(BF16))
        a_chunk.append(jnp.exp(b_last))
        yield
    kcols = lambda h: slice(h * GLA_DK, (h + 1) * GLA_DK)
    vcols = lambda h: slice(h * GLA_DV, (h + 1) * GLA_DV)
    vh = [v_ref[0, ci * CHUNK:(ci + 1) * CHUNK, vcols(h)] for ci, h in probs]
    scores, o_intra, d_st = [], [], []
    for v, (ci, h) in zip(vh, probs):
        s = _dot_nt(q_in[ci][:, kcols(h)], k_neg[ci][:, kcols(h)])
        scores.append(jnp.where(causal, s, 0.0).astype(BF16))
        d_st.append(_dot_tn(v, k_state[ci][:, kcols(h)]))
        if h % 2 == 1:
            yield
    for s, v, (ci, h) in zip(scores, vh, probs):
        o_intra.append(_dot(s, v))
        if h % 2 == 1:
            yield

    for i, (ci, h) in enumerate(probs):
        rows = slice(ci * CHUNK, (ci + 1) * CHUNK)
        st = st_ref[h]
        o = o_intra[i] + _dot_nt(q_in[ci][:, kcols(h)], st.astype(BF16))
        st_ref[h] = st * a_chunk[ci][:, kcols(h)] + d_st[i]
        o = o * lax.rsqrt(jnp.mean(o * o, axis=-1, keepdims=True) + EPS) * hnw
        o_ref[0, rows, vcols(h)] = (o * gate_ref[0, rows, vcols(h)].astype(F32)).astype(o_ref.dtype)
        if h % 2 == 1:
            yield


def _interleave(*streams):
    streams = list(streams)
    while streams:
        for g in list(streams):
            if next(g, StopIteration) is StopIteration:
                streams.remove(g)


def _gla_body(qk_ref, v_ref, sm_ref, gate_ref, wgg_ref, bgg_ref, hnw_ref, o_ref, st_ref):
    @pl.when(pl.program_id(1) == 0)
    def _():
        st_ref[...] = jnp.zeros_like(st_ref)

    _interleave(_gla_stream(qk_ref, v_ref, sm_ref, gate_ref, wgg_ref, bgg_ref, hnw_ref, o_ref, st_ref))


def _gla(plain, small, gates, wgg, bgg, hnw, b, t):
    tt = min(MIX_TT, t)
    return pl.pallas_call(
        _gla_body,
        grid=(b, t // tt),
        in_specs=[
            pl.BlockSpec((1, tt, 2 * GLA_QK), lambda bi, ti: (bi, ti, 0)),
            pl.BlockSpec((1, tt, GLA_V), lambda bi, ti: (bi, ti, 1)),
            pl.BlockSpec((1, tt, SMALL_W), lambda bi, ti: (bi, ti, 0)),
            pl.BlockSpec((1, tt, GLA_V), lambda bi, ti: (bi, ti, 0)),
            pl.BlockSpec((SMALL_W, GLA_QK), lambda bi, ti: (0, 0)),
            pl.BlockSpec((1, GLA_QK), lambda bi, ti: (0, 0)),
            pl.BlockSpec((1, GLA_DV), lambda bi, ti: (0, 0)),
        ],
        out_specs=pl.BlockSpec((1, tt, GLA_V), lambda bi, ti: (bi, ti, 0)),
        out_shape=jax.ShapeDtypeStruct((b, t, GLA_V), BF16),
        scratch_shapes=[pltpu.VMEM((GLA_HEADS, GLA_DV, GLA_DK), F32)],
        compiler_params=_cparams(("parallel", "arbitrary")),
        name="gla",
    )(plain, plain, small, gates, wgg, bgg, hnw)


def _gdn_body(q_ref, k_ref, v_ref, sm_ref, smt_ref, gate_ref, par_ref, part_ref, hnw_ref,
              o_ref, s_ref, u_ref, wq_ref, qk_ref, ks_ref, side_a=(), side_b=()):
    nb, tt = q_ref.shape[0], q_ref.shape[1]
    nc = tt // CHUNK
    heads = range(DN_HEADS)

    @pl.when(pl.program_id(1) == 0)
    def _():
        s_ref[...] = jnp.zeros_like(s_ref)

    causal, strict, _ = _tri_masks()
    r = lax.broadcasted_iota(jnp.int32, (tt, tt), 0)
    c = lax.broadcasted_iota(jnp.int32, (tt, tt), 1)
    same_chunk = (r // CHUNK) == (c // CHUNK)
    bd_tril = jnp.where(same_chunk & (r >= c), 1.0, 0.0).astype(F32)
    bd_triu = jnp.where(same_chunk & (r <= c), 1.0, 0.0).astype(F32)

    beta_all, g_col_all, g_row_all = [], [], []
    for bi in range(nb):
        sm = sm_ref[bi]
        beta_all.append(_sigmoid(sm))
        g_all = -jnp.exp(par_ref[0:1, :]) * _softplus(sm + par_ref[1:2, :])
        smt = smt_ref[bi]
        gt_all = -jnp.exp(part_ref[:, 0:1]) * _softplus(smt + part_ref[:, 1:2])
        g_col_all.append(_ones_dot_rhs(bd_tril, g_all))
        g_row_all.append(_ones_dot_lhs(gt_all, bd_triu))
    hnw = hnw_ref[...]

    ri = lax.broadcasted_iota(jnp.int32, (CHUNK, CHUNK), 0)
    ci_ = lax.broadcasted_iota(jnp.int32, (CHUNK, CHUNK), 1)
    eye = jnp.where(ri == ci_, 1.0, 0.0).astype(F32)

    def phase_a(chunks):
        probs = [(bi, ci, h) for bi in range(nb) for ci in chunks for h in heads]
        kb, decay, exp_g, kq = [], [], [], []
        for bi, ci, h in probs:
            rows = slice(ci * CHUNK, (ci + 1) * CHUNK)
            q16 = q_ref[bi, rows, h * DN_DK:(h + 1) * DN_DK]
            k16 = k_ref[bi, rows, h * DN_DK:(h + 1) * DN_DK]
            k = k16.astype(F32)
            beta = beta_all[bi][rows, LANE_BETA + h:LANE_BETA + h + 1]
            g_col = g_col_all[bi][rows, LANE_DA + h:LANE_DA + h + 1]
            g_row = g_row_all[bi][LANE_DA + h:LANE_DA + h + 1, rows]
            g_last = g_row[:, CHUNK - 1:CHUNK]
            kb.append(k * beta)
            decay.append(jnp.exp(jnp.where(causal, g_col - g_row, -jnp.inf)))
            exp_g.append(jnp.exp(g_col))
            wq_ref[bi, ci, h, CHUNK:, :] = (q16.astype(F32) * exp_g[-1]).astype(BF16)
            ks_ref[bi, ci, h] = (k * jnp.exp(g_last - g_col)).astype(BF16)
            kq.append(_dot_nt(jnp.concatenate([kb[-1].astype(BF16), q16], axis=0), k16))
        yield
        p = [-jnp.where(strict, x[0:CHUNK] * d, 0.0) for x, d in zip(kq, decay)]
        inv = [eye + x for x in p]
        size = 2
        while size < CHUNK:
            p16 = [x.astype(BF16) for x in p]
            p = [_dot(x, x) for x in p16]
            yield
            inv = [t + _dot(t.astype(BF16), x.astype(BF16)) for t, x in zip(inv, p)]
            yield
            size *= 2
        for i, (bi, ci, h) in enumerate(probs):
            rows = slice(ci * CHUNK, (ci + 1) * CHUNK)
            beta = beta_all[bi][rows, LANE_BETA + h:LANE_BETA + h + 1]
            vb = v_ref[bi, rows, h * DN_DV:(h + 1) * DN_DV].astype(F32) * beta
            rhs = jnp.concatenate([vb, kb[i] * exp_g[i]], axis=1).astype(BF16)
            uw = _dot(inv[i].astype(BF16), rhs)
            u_ref[bi, ci, h] = uw[:, 0:DN_DV]
            wq_ref[bi, ci, h, 0:CHUNK, :] = uw[:, DN_DV:].astype(BF16)
            qk_ref[bi, ci, h] = (kq[i][CHUNK:] * decay[i]).astype(BF16)
        yield

    def phase_b(chunks):
        seqs = [(bi, h) for bi in range(nb) for h in heads]
        for ci in chunks:
            rows = slice(ci * CHUNK, (ci + 1) * CHUNK)
            last = ci * CHUNK + CHUNK - 1
            s = [s_ref[bi, h] for bi, h in seqs]
            ws = []
            for i, ((bi, h), x) in enumerate(zip(seqs, s)):
                ws.append(_dot(wq_ref[bi, ci, h], x.astype(BF16)))
                if i % GDN_YIELD == GDN_YIELD - 1:
                    yield
            vn16, o = [], []
            for i, ((bi, h), x, y) in enumerate(zip(seqs, s, ws)):
                vn = (u_ref[bi, ci, h] - y[0:CHUNK]).astype(BF16)
                vn16.append(vn)
                o.append(y[CHUNK:] + _dot(qk_ref[bi, ci, h], vn))
                g_chunk = jnp.exp(g_row_all[bi][LANE_DA + h:LANE_DA + h + 1, last:last + 1])
                s_ref[bi, h] = x * g_chunk + _dot_tn(ks_ref[bi, ci, h], vn)
                if i % GDN_YIELD == GDN_YIELD - 1:
                    yield
            for i, ((bi, h), x) in enumerate(zip(seqs, o)):
                cols = slice(h * DN_DV, (h + 1) * DN_DV)
                on = x * lax.rsqrt(jnp.mean(x * x, axis=-1, keepdims=True) + EPS) * hnw
                o_ref[bi, rows, cols] = (on * gate_ref[bi, rows, cols].astype(F32)).astype(o_ref.dtype)
                if i % GDN_YIELD == GDN_YIELD - 1:
                    yield

    _interleave(phase_a(range(nc)), *side_a)
    _interleave(phase_b(range(nc)), *side_b)


def _gdn(plain, small, small_t, gates, par, par_t, hnw, b, t):
    tt = min(GDN_TT, t)
    nb = GDN_NB if b % GDN_NB == 0 else 1
    nc = tt // CHUNK
    return pl.pallas_call(
        _gdn_body,
        grid=(b // nb, t // tt),
        in_specs=[
            pl.BlockSpec((nb, tt, DN_QK), lambda bi, ti: (bi, ti, 2)),
            pl.BlockSpec((nb, tt, DN_QK), lambda bi, ti: (bi, ti, 3)),
            pl.BlockSpec((nb, tt, DN_V), lambda bi, ti: (bi, ti, 4)),
            pl.BlockSpec((nb, tt, SMALL_W), lambda bi, ti: (bi, ti, 0)),
            pl.BlockSpec((nb, SMALL_T_ROWS, tt), lambda bi, ti: (bi, 0, ti)),
            pl.BlockSpec((nb, tt, DN_V), lambda bi, ti: (bi, ti, 1)),
            pl.BlockSpec((2, SMALL_W), lambda bi, ti: (0, 0)),
            pl.BlockSpec((SMALL_T_ROWS, 2), lambda bi, ti: (0, 0)),
            pl.BlockSpec((1, DN_DV), lambda bi, ti: (0, 0)),
        ],
        out_specs=pl.BlockSpec((nb, tt, DN_V), lambda bi, ti: (bi, ti, 0)),
        out_shape=jax.ShapeDtypeStruct((b, t, DN_V), BF16),
        scratch_shapes=[
            pltpu.VMEM((nb, DN_HEADS, DN_DK, DN_DV), F32),
            pltpu.VMEM((nb, nc, DN_HEADS, CHUNK, DN_DV), F32),
            pltpu.VMEM((nb, nc, DN_HEADS, 2 * CHUNK, DN_DK), BF16),
            pltpu.VMEM((nb, nc, DN_HEADS, CHUNK, CHUNK), BF16),
            pltpu.VMEM((nb, nc, DN_HEADS, CHUNK, DN_DK), BF16),
        ],
        compiler_params=_cparams(("parallel", "arbitrary")),
        name="gdn",
    )(plain, plain, plain, small, small_t, gates, par, par_t, hnw)


def _mixers_body(*refs):
    (gla_qk, gla_v, gla_sm, gla_gate, wgg, bgg, gla_hnw,
     dn_q, dn_k, dn_v, dn_sm, dn_smt, dn_gate, par, part, dn_hnw,
     ya_ref, yb_ref, gla_st, dn_s, dn_u, dn_wq, dn_qk, dn_ks) = refs
    @pl.when(pl.program_id(1) == 0)
    def _():
        gla_st[...] = jnp.zeros_like(gla_st)

    gla = _gla_stream(gla_qk, gla_v, gla_sm, gla_gate, wgg, bgg, gla_hnw, ya_ref, gla_st)
    _gdn_body(dn_q, dn_k, dn_v, dn_sm, dn_smt, dn_gate, par, part, dn_hnw, yb_ref,
              dn_s, dn_u, dn_wq, dn_qk, dn_ks, side_b=(gla,))


def _mixers(plain, small, small_t, gates, wgg, bgg, gla_hnw, par, par_t, dn_hnw, b, t):
    tt = min(MIX_TT, t)
    nc = tt // CHUNK
    blk = lambda w, j: pl.BlockSpec((1, tt, w), lambda bi, ti: (bi, ti, j))
    const = lambda shape: pl.BlockSpec(shape, lambda bi, ti: (0,) * len(shape))
    return pl.pallas_call(
        _mixers_body,
        grid=(b, t // tt),
        in_specs=[
            blk(2 * GLA_QK, 0), blk(GLA_V, 1), blk(SMALL_W, 0), blk(GLA_V, 0),
            const((SMALL_W, GLA_QK)), const((1, GLA_QK)), const((1, GLA_DV)),
            blk(DN_QK, 2), blk(DN_QK, 3), blk(DN_V, 4), blk(SMALL_W, 0),
            pl.BlockSpec((1, SMALL_T_ROWS, tt), lambda bi, ti: (bi, 0, ti)),
            blk(DN_V, 1),
            const((2, SMALL_W)), const((SMALL_T_ROWS, 2)), const((1, DN_DV)),
        ],
        out_specs=[blk(GLA_V, 0), blk(DN_V, 0)],
        out_shape=[jax.ShapeDtypeStruct((b, t, GLA_V), BF16), jax.ShapeDtypeStruct((b, t, DN_V), BF16)],
        scratch_shapes=[
            pltpu.VMEM((GLA_HEADS, GLA_DV, GLA_DK), F32),
            pltpu.VMEM((1, DN_HEADS, DN_DK, DN_DV), F32),
            pltpu.VMEM((1, nc, DN_HEADS, CHUNK, DN_DV), F32),
            pltpu.VMEM((1, nc, DN_HEADS, 2 * CHUNK, DN_DK), BF16),
            pltpu.VMEM((1, nc, DN_HEADS, CHUNK, CHUNK), BF16),
            pltpu.VMEM((1, nc, DN_HEADS, CHUNK, DN_DK), BF16),
        ],
        compiler_params=_cparams(("parallel", "arbitrary")),
        name="mixers",
    )(plain, plain, small, gates, wgg, bgg, gla_hnw,
      plain, plain, plain, small, small_t, gates, par, par_t, dn_hnw)


def _layer(h, ffn1_norm, ffn1_w_gate, ffn1_w_up, ffn1_w_down, mix_norm, w_in, w_gla_gate, b_gla_gate,
           conv_w, dn_a_log, dn_dt_bias, gla_head_norm, dn_head_norm, w_out, ffn2_norm, ffn2_w_gate,
           ffn2_w_up, ffn2_w_down, final_w, final_norm):
    b, t, d = h.shape
    m = b * t
    row = lambda a: a.reshape(1, -1).astype(F32)

    h1 = _ffn(h.reshape(m, d), row(ffn1_norm), ffn1_w_gate.astype(BF16), ffn1_w_up.astype(BF16),
              ffn1_w_down.astype(BF16))

    sizes = (GLA_QK, GLA_QK, GLA_V, GLA_V, GLA_RANK, DN_QK, DN_QK, DN_V, DN_V, DN_HEADS, DN_HEADS,
             D_MODEL, D_MODEL)
    offs = [0]
    for s in sizes:
        offs.append(offs[-1] + s)
    col = lambda i: w_in[:, offs[i]:offs[i + 1]]
    (w_gq, w_gk, w_gv, w_gr, w_glr, w_dq, w_dk, w_dv, w_dgate, w_dbeta, w_da, w_ma, w_mb) = (
        col(i) for i in range(len(sizes)))
    w_plain = jnp.concatenate([w_gq, w_gk, w_gv, w_dq, w_dk, w_dv], axis=1).astype(BF16)
    w_gate_a = jnp.concatenate([w_gr, w_dgate], axis=1).astype(BF16)
    w_gate_b = jnp.concatenate([w_ma, w_mb], axis=1).astype(BF16)
    w_small = jnp.concatenate(
        [w_glr, w_dbeta, w_da, jnp.zeros((d, SMALL_W - LANE_DA - DN_HEADS), F32)], axis=1).astype(BF16)
    w_small_t = w_small[:, :SMALL_T_ROWS].T

    mixn = row(mix_norm)
    plain, gates, small, small_t = _proj(h1.reshape(b, t, d), mixn, w_plain, w_gate_a, w_gate_b, w_small,
                                         w_small_t, conv_w.astype(F32))
    plain, gates, small = (a.reshape(b, t, -1) for a in (plain, gates, small))

    wgg = jnp.zeros((SMALL_W, GLA_QK), F32).at[LANE_GLR:LANE_GLR + GLA_RANK].set(w_gla_gate).astype(BF16)
    par = jnp.zeros((2, SMALL_W), F32)
    par = par.at[0, LANE_DA:LANE_DA + DN_HEADS].set(dn_a_log.astype(F32))
    par = par.at[1, LANE_DA:LANE_DA + DN_HEADS].set(dn_dt_bias.astype(F32))
    par_t = par[:, :SMALL_T_ROWS].T
    ya, yb = _mixers(plain, small, small_t, gates, wgg, row(b_gla_gate), row(gla_head_norm), par, par_t,
                     row(dn_head_norm), b, t)

    h3 = _mix_out_ffn(h1, ya.reshape(m, d), yb.reshape(m, d), w_out.astype(BF16), row(ffn2_norm),
                      ffn2_w_gate.astype(BF16), ffn2_w_up.astype(BF16), ffn2_w_down.astype(BF16),
                      row(final_w), final_norm)
    return h3.reshape(b, t, d)


def kernel(x, ffn1_norm, ffn1_w_gate, ffn1_w_up, ffn1_w_down, mix_norm, w_in, w_gla_gate, b_gla_gate,
           conv_w, dn_a_log, dn_dt_bias, gla_head_norm, dn_head_norm, w_out, ffn2_norm, ffn2_w_gate,
           ffn2_w_up, ffn2_w_down, final_norm):
    depth = ffn1_norm.shape[0]
    h = x
    for layer in range(depth):
        h = _layer(h, ffn1_norm[layer], ffn1_w_gate[layer], ffn1_w_up[layer], ffn1_w_down[layer],
                   mix_norm[layer], w_in[layer], w_gla_gate[layer], b_gla_gate[layer], conv_w[layer],
                   dn_a_log[layer], dn_dt_bias[layer], gla_head_norm[layer], dn_head_norm[layer],
                   w_out[layer], ffn2_norm[layer], ffn2_w_gate[layer], ffn2_w_up[layer],
                   ffn2_w_down[layer], final_norm, layer == depth - 1)
    return h
```

```python
import functools
import itertools

import jax
import jax.numpy as jnp
from jax import lax
from jax.experimental import pallas as pl
from jax.experimental.pallas import tpu as pltpu

F32 = jnp.float32
BF16 = jnp.bfloat16

D_MODEL = 1024
D_FF = 2816
FFN_RES = 0.5
GLA_HEADS = 4
GLA_DK = 128
GLA_DV = 256
GLA_RANK = 16
GLA_TAU = 16.0
DN_HEADS = 8
DN_DK = 128
DN_DV = 128
CONV_K = 4
CHUNK = 64
EPS = 1e-6

GLA_QK = GLA_HEADS * GLA_DK
GLA_V = GLA_HEADS * GLA_DV
DN_QK = DN_HEADS * DN_DK
DN_V = DN_HEADS * DN_DV

SMALL_W = 128
SMALL_T_ROWS = 32
LANE_GLR = 0
LANE_BETA = GLA_RANK
LANE_DA = GLA_RANK + DN_HEADS

FFN_TM = 512
FFN_TF = 256
PROJ_TM = 512
PROJ_TN = 256
CONV_TN = 256
CONV_STRIP = 128
CONV_PIECE = 1
CONV_SLOTS = 4
CONV_PAD = 128
MIX_TT = 256
GDN_TT = 256
GDN_NB = 1
GDN_YIELD = 8
CONV_HALO = 8

VMEM_LIMIT_BYTES = 48 * 1024 * 1024


def _cparams(sem, flags=None):
    return pltpu.CompilerParams(dimension_semantics=sem, vmem_limit_bytes=VMEM_LIMIT_BYTES, flags=flags)


def _rms(x, w):
    return x * lax.rsqrt(jnp.mean(x * x, axis=-1, keepdims=True) + EPS) * w


def _sigmoid(x):
    return jax.nn.sigmoid(x)


def _silu(x):
    return x * _sigmoid(x)


def _softplus(x):
    return jnp.maximum(x, 0.0) + jnp.log1p(jnp.exp(-jnp.abs(x)))


def _dot(a, b):
    return jnp.dot(a, b, preferred_element_type=F32)


def _dot_nt(a, b):
    return lax.dot_general(a, b, (((1,), (1,)), ((), ())), preferred_element_type=F32)


def _dot_tn(a, b):
    return lax.dot_general(a, b, (((0,), (0,)), ((), ())), preferred_element_type=F32)


def _split3(x):
    hi = x.astype(BF16)
    r1 = x - hi.astype(F32)
    mid = r1.astype(BF16)
    lo = (r1 - mid.astype(F32)).astype(BF16)
    return hi, mid, lo


def _ones_dot_rhs(ones, x):
    n = x.shape[1]
    y = _dot(ones.astype(BF16), jnp.concatenate(_split3(x), axis=1))
    return y[:, 0:n] + y[:, n:2 * n] + y[:, 2 * n:]


def _ones_dot_lhs(x, ones):
    m = x.shape[0]
    y = _dot(jnp.concatenate(_split3(x), axis=0), ones.astype(BF16))
    return y[0:m] + y[m:2 * m] + y[2 * m:]


def _swiglu_half_step(x, nw_ref, wg_ref, wu_ref, wd_ref):
    xn = _rms(x, nw_ref[...]).astype(BF16)
    ff = wg_ref.shape[1]
    acc = None
    for f0 in range(0, ff, FFN_TF):
        g = _dot(xn, wg_ref[:, f0:f0 + FFN_TF])
        u = _dot(xn, wu_ref[:, f0:f0 + FFN_TF])
        part = _dot((_silu(g) * u).astype(BF16), wd_ref[f0:f0 + FFN_TF, :])
        acc = part if acc is None else acc + part
    return x + FFN_RES * acc


def _ffn_body(x_ref, nw_ref, wg_ref, wu_ref, wd_ref, o_ref):
    o_ref[...] = _swiglu_half_step(x_ref[...], nw_ref, wg_ref, wu_ref, wd_ref)


def _mix_out_ffn_body(h_ref, ya_ref, yb_ref, wo_ref, nw_ref, wg_ref, wu_ref, wd_ref, fw_ref, o_ref, *,
                      final_norm):
    y = (ya_ref[...].astype(F32) + yb_ref[...].astype(F32)).astype(BF16)
    h = _swiglu_half_step(h_ref[...] + _dot(y, wo_ref[...]), nw_ref, wg_ref, wu_ref, wd_ref)
    if final_norm:
        h = _rms(h, fw_ref[...])
    o_ref[...] = h


def _resident(shape):
    return pl.BlockSpec(shape, lambda *_: (0,) * len(shape), pipeline_mode=pl.Buffered(1))


def _ffn(x2d, norm_w, wg, wu, wd):
    m, d = x2d.shape
    ff = wg.shape[1]
    tm = min(FFN_TM, m)
    row = pl.BlockSpec((tm, d), lambda i: (i, 0))
    return pl.pallas_call(
        _ffn_body,
        grid=(m // tm,),
        in_specs=[row, _resident((1, d)), _resident((d, ff)), _resident((d, ff)), _resident((ff, d))],
        out_specs=row,
        out_shape=jax.ShapeDtypeStruct((m, d), F32),
        compiler_params=_cparams(("parallel",)),
        name="ffn",
    )(x2d, norm_w, wg, wu, wd)


def _mix_out_ffn(h2d, ya, yb, wo, norm_w, wg, wu, wd, final_w, final_norm):
    m, d = h2d.shape
    ff = wg.shape[1]
    tm = min(FFN_TM, m)
    row = pl.BlockSpec((tm, d), lambda i: (i, 0))
    return pl.pallas_call(
        functools.partial(_mix_out_ffn_body, final_norm=final_norm),
        grid=(m // tm,),
        in_specs=[row, row, row, _resident((d, d)), _resident((1, d)), _resident((d, ff)),
                  _resident((d, ff)), _resident((ff, d)), _resident((1, d))],
        out_specs=row,
        out_shape=jax.ShapeDtypeStruct((m, d), F32),
        compiler_params=_cparams(("parallel",)),
        name="mix_out_ffn",
    )(h2d, ya, yb, wo, norm_w, wg, wu, wd, final_w)


def _proj_body(x_ref, nw_ref, wp_ref, wa_ref, wb_ref, ws_ref, wst_ref, cw_ref, plain_ref, gate_ref, sm_ref,
               smt_ref, carry_ref, pe_ref):
    tm = x_ref.shape[0]
    n_gla = 2 * GLA_QK + GLA_V
    n_dn_qk = 2 * DN_QK

    @pl.when(pl.program_id(1) == 0)
    def _():
        carry_ref[...] = jnp.zeros_like(carry_ref)

    xn = _rms(x_ref[...], nw_ref[...]).astype(BF16)

    def plain_chunk(n0):
        cols = slice(n0, n0 + PROJ_TN)
        plain_ref[:, cols] = _dot(xn, wp_ref[:, cols]).astype(plain_ref.dtype)

    def gate_chunk(n0):
        cols = slice(n0, n0 + PROJ_TN)
        gate_ref[:, cols] = (_silu(_dot(xn, wa_ref[:, cols])) *
                             _sigmoid(_dot(xn, wb_ref[:, cols]))).astype(gate_ref.dtype)

    def conv_dot(ic):
        c0 = ic * CONV_TN
        ccols = slice(c0, c0 + CONV_TN)
        pe = pe_ref.at[ic % pe_ref.shape[0]]
        pe[0:CONV_HALO, 0:CONV_TN] = carry_ref[:, ccols]
        pe[CONV_HALO:, 0:CONV_TN] = _dot(xn, wp_ref[:, n_gla + c0:n_gla + c0 + CONV_TN])
        carry_ref[:, ccols] = pe[tm:tm + CONV_HALO, 0:CONV_TN]

    def conv_strips(ic):
        c0 = ic * CONV_TN
        ccols = slice(c0, c0 + CONV_TN)
        pe = pe_ref.at[ic % pe_ref.shape[0]]
        for r0 in range(0, tm, CONV_STRIP):
            win = pe[r0:r0 + CONV_STRIP + CONV_HALO, 0:CONV_TN]
            w0, w1, w2, w3 = (cw_ref[j:j + 1, ccols] for j in range(CONV_K))
            prev = pltpu.roll(win, 1, axis=0)
            y = (w3 * win + w2 * prev) + pltpu.roll(w1 * win + w0 * prev, 2, axis=0)
            y = _silu(y[CONV_HALO:, :])
            if c0 < n_dn_qk:
                scale = DN_DK ** -0.5 if c0 < DN_QK else 1.0
                heads = [y[:, h0:h0 + DN_DK] for h0 in range(0, CONV_TN, DN_DK)]
                y = jnp.concatenate(
                    [yh * (lax.rsqrt(jnp.sum(yh * yh, axis=-1, keepdims=True) + EPS) * scale)
                     for yh in heads], axis=1)
            plain_ref[r0:r0 + CONV_STRIP, n_gla + c0:n_gla + c0 + CONV_TN] = y.astype(plain_ref.dtype)
            if (r0 // CONV_STRIP) % CONV_PIECE == CONV_PIECE - 1:
                yield

    def mxu_pieces(tasks):
        for task in tasks:
            task()
            yield

    plain_tasks = [functools.partial(plain_chunk, n0) for n0 in range(0, n_gla, PROJ_TN)]
    gate_tasks = [functools.partial(gate_chunk, n0) for n0 in range(0, wa_ref.shape[1], PROJ_TN)]
    others = [t for pair in itertools.zip_longest(plain_tasks, gate_tasks) for t in pair if t is not None]
    n_conv = (wp_ref.shape[1] - n_gla) // CONV_TN
    conv_dot(0)
    for ic in range(n_conv):
        lo, hi = ic * len(others) // n_conv, (ic + 1) * len(others) // n_conv
        if ic + 1 < n_conv:
            conv_dot(ic + 1)
        _interleave(conv_strips(ic), mxu_pieces(others[lo:hi]))
    sm_ref[...] = _dot(xn, ws_ref[...])
    smt_ref[0] = _dot_nt(wst_ref[...], xn)


def _proj(h3d, norm_w, w_plain, w_gate_a, w_gate_b, w_small, w_small_t, conv_w):
    b, t, d = h3d.shape
    tm = min(PROJ_TM, t)
    nt = t // tm
    n_plain, n_gate, n_conv = w_plain.shape[1], w_gate_a.shape[1], conv_w.shape[1]
    rows = lambda n: pl.BlockSpec((tm, n), lambda bi, ti: (bi * nt + ti, 0))
    return pl.pallas_call(
        _proj_body,
        grid=(b, nt),
        in_specs=[rows(d), _resident((1, d)), _resident((d, n_plain)), _resident((d, n_gate)),
                  _resident((d, n_gate)), _resident((d, SMALL_W)), _resident((SMALL_T_ROWS, d)),
                  _resident((CONV_K, n_conv))],
        out_specs=[rows(n_plain), rows(n_gate), rows(SMALL_W),
                   pl.BlockSpec((1, SMALL_T_ROWS, tm), lambda bi, ti: (bi, 0, ti))],
        out_shape=[
            jax.ShapeDtypeStruct((b * t, n_plain), BF16),
            jax.ShapeDtypeStruct((b * t, n_gate), BF16),
            jax.ShapeDtypeStruct((b * t, SMALL_W), F32),
            jax.ShapeDtypeStruct((b, SMALL_T_ROWS, t), F32),
        ],
        scratch_shapes=[pltpu.VMEM((CONV_HALO, n_conv), F32),
                        pltpu.VMEM((CONV_SLOTS, CONV_HALO + tm, CONV_TN + CONV_PAD), F32)],
        compiler_params=_cparams(("parallel", "arbitrary")),
        name="proj",
    )(h3d.reshape(b * t, d), norm_w, w_plain, w_gate_a, w_gate_b, w_small, w_small_t, conv_w)


def _tri_masks():
    r = lax.broadcasted_iota(jnp.int32, (CHUNK, CHUNK), 0)
    c = lax.broadcasted_iota(jnp.int32, (CHUNK, CHUNK), 1)
    return r >= c, r > c, r <= c


def _gla_stream(qk_ref, v_ref, sm_ref, gate_ref, wgg_ref, bgg_ref, hnw_ref, o_ref, st_ref):
    causal, _, _ = _tri_masks()
    tt = qk_ref.shape[1]
    nc = tt // CHUNK
    r = lax.broadcasted_iota(jnp.int32, (tt, tt), 0)
    c = lax.broadcasted_iota(jnp.int32, (tt, tt), 1)
    bd_tril = jnp.where(((r // CHUNK) == (c // CHUNK)) & (r >= c), 1.0, 0.0).astype(F32)

    log_a = []
    for ci in range(nc):
        z = _dot(sm_ref[0, ci * CHUNK:(ci + 1) * CHUNK, :].astype(BF16), wgg_ref[...]) + bgg_ref[...]
        log_a.append((jnp.minimum(z, 0.0) - jnp.log1p(jnp.exp(-jnp.abs(z)))) * (1.0 / GLA_TAU))
        yield
    b = _ones_dot_rhs(bd_tril, jnp.concatenate(log_a, axis=0))
    hnw = hnw_ref[...]
    yield

    probs = [(ci, h) for ci in range(nc) for h in range(GLA_HEADS)]
    q_in, k_neg, k_state, a_chunk = [], [], [], []
    for ci in range(nc):
        rows = slice(ci * CHUNK, (ci + 1) * CHUNK)
        bc = b[rows]
        b_last = bc[CHUNK - 1:CHUNK, :]
        q = qk_ref[0, rows, 0:GLA_QK].astype(F32) * (GLA_DK ** -0.5)
        k = qk_ref[0, rows, GLA_QK:2 * GLA_QK].astype(F32)
        q_in.append((q * jnp.exp(bc)).astype(BF16))
        k_neg.append((k * jnp.exp(-bc)).astype(BF16))
        k_state.append((k * jnp.exp(b_last - bc)).astype(BF16))
        a_chunk.append(jnp.exp(b_last))
        yield
    kcols = lambda h: slice(h * GLA_DK, (h + 1) * GLA_DK)
    vcols = lambda h: slice(h * GLA_DV, (h + 1) * GLA_DV)
    vh = [v_ref[0, ci * CHUNK:(ci + 1) * CHUNK, vcols(h)] for ci, h in probs]
    scores, o_intra, d_st = [], [], []
    for v, (ci, h) in zip(vh, probs):
        s = _dot_nt(q_in[ci][:, kcols(h)], k_neg[ci][:, kcols(h)])
        scores.append(jnp.where(causal, s, 0.0).astype(BF16))
        d_st.append(_dot_tn(v, k_state[ci][:, kcols(h)]))
        if h % 2 == 1:
            yield
    for s, v, (ci, h) in zip(scores, vh, probs):
        o_intra.append(_dot(s, v))
        if h % 2 == 1:
            yield

    for i, (ci, h) in enumerate(probs):
        rows = slice(ci * CHUNK, (ci + 1) * CHUNK)
        st = st_ref[h]
        o = o_intra[i] + _dot_nt(q_in[ci][:, kcols(h)], st.astype(BF16))
        st_ref[h] = st * a_chunk[ci][:, kcols(h)] + d_st[i]
        o = o * lax.rsqrt(jnp.mean(o * o, axis=-1, keepdims=True) + EPS) * hnw
        o_ref[0, rows, vcols(h)] = (o * gate_ref[0, rows, vcols(h)].astype(F32)).astype(o_ref.dtype)
        if h % 2 == 1:
            yield


def _interleave(*streams):
    streams = list(streams)
    while streams:
        for g in list(streams):
            if next(g, StopIteration) is StopIteration:
                streams.remove(g)


def _gla_body(qk_ref, v_ref, sm_ref, gate_ref, wgg_ref, bgg_ref, hnw_ref, o_ref, st_ref):
    @pl.when(pl.program_id(1) == 0)
    def _():
        st_ref[...] = jnp.zeros_like(st_ref)

    _interleave(_gla_stream(qk_ref, v_ref, sm_ref, gate_ref, wgg_ref, bgg_ref, hnw_ref, o_ref, st_ref))


def _gla(plain, small, gates, wgg, bgg, hnw, b, t):
    tt = min(MIX_TT, t)
    return pl.pallas_call(
        _gla_body,
        grid=(b, t // tt),
        in_specs=[
            pl.BlockSpec((1, tt, 2 * GLA_QK), lambda bi, ti: (bi, ti, 0)),
            pl.BlockSpec((1, tt, GLA_V), lambda bi, ti: (bi, ti, 1)),
            pl.BlockSpec((1, tt, SMALL_W), lambda bi, ti: (bi, ti, 0)),
            pl.BlockSpec((1, tt, GLA_V), lambda bi, ti: (bi, ti, 0)),
            pl.BlockSpec((SMALL_W, GLA_QK), lambda bi, ti: (0, 0)),
            pl.BlockSpec((1, GLA_QK), lambda bi, ti: (0, 0)),
            pl.BlockSpec((1, GLA_DV), lambda bi, ti: (0, 0)),
        ],
        out_specs=pl.BlockSpec((1, tt, GLA_V), lambda bi, ti: (bi, ti, 0)),
        out_shape=jax.ShapeDtypeStruct((b, t, GLA_V), BF16),
        scratch_shapes=[pltpu.VMEM((GLA_HEADS, GLA_DV, GLA_DK), F32)],
        compiler_params=_cparams(("parallel", "arbitrary")),
        name="gla",
    )(plain, plain, small, gates, wgg, bgg, hnw)


def _gdn_body(q_ref, k_ref, v_ref, sm_ref, smt_ref, gate_ref, par_ref, part_ref, hnw_ref,
              o_ref, s_ref, u_ref, wq_ref, qk_ref, ks_ref, side_0=(), side_a=(), side_b=()):
    nb, tt = q_ref.shape[0], q_ref.shape[1]
    nc = tt // CHUNK
    heads = range(DN_HEADS)

    @pl.when(pl.program_id(1) == 0)
    def _():
        s_ref[...] = jnp.zeros_like(s_ref)

    causal, strict, _ = _tri_masks()
    r = lax.broadcasted_iota(jnp.int32, (tt, tt), 0)
    c = lax.broadcasted_iota(jnp.int32, (tt, tt), 1)
    same_chunk = (r // CHUNK) == (c // CHUNK)
    bd_tril = jnp.where(same_chunk & (r >= c), 1.0, 0.0).astype(F32)
    bd_triu = jnp.where(same_chunk & (r <= c), 1.0, 0.0).astype(F32)

    beta_all, g_col_all, g_row_all = [], [], []
    for bi in range(nb):
        sm = sm_ref[bi]
        beta_all.append(_sigmoid(sm))
        g_all = -jnp.exp(par_ref[0:1, :]) * _softplus(sm + par_ref[1:2, :])
        smt = smt_ref[bi]
        gt_all = -jnp.exp(part_ref[:, 0:1]) * _softplus(smt + part_ref[:, 1:2])
        g_col_all.append(_ones_dot_rhs(bd_tril, g_all))
        g_row_all.append(_ones_dot_lhs(gt_all, bd_triu))
    hnw = hnw_ref[...]

    ri = lax.broadcasted_iota(jnp.int32, (CHUNK, 2 * CHUNK), 0)
    li = lax.broadcasted_iota(jnp.int32, (CHUNK, 2 * CHUNK), 1)
    left = li < CHUNK
    eye2 = jnp.where((ri == li) | (ri + CHUNK == li), 1.0, 0.0).astype(F32)

    def phase_a(chunks):
        probs = [(bi, ci, h) for bi in range(nb) for ci in chunks for h in heads]
        kb, decay, exp_g, kq = [], [], [], []
        for bi, ci, h in probs:
            rows = slice(ci * CHUNK, (ci + 1) * CHUNK)
            q16 = q_ref[bi, rows, h * DN_DK:(h + 1) * DN_DK]
            k16 = k_ref[bi, rows, h * DN_DK:(h + 1) * DN_DK]
            k = k16.astype(F32)
            beta = beta_all[bi][rows, LANE_BETA + h:LANE_BETA + h + 1]
            g_col = g_col_all[bi][rows, LANE_DA + h:LANE_DA + h + 1]
            g_row = g_row_all[bi][LANE_DA + h:LANE_DA + h + 1, rows]
            g_last = g_row[:, CHUNK - 1:CHUNK]
            kb.append(k * beta)
            decay.append(jnp.exp(jnp.where(causal, g_col - g_row, -jnp.inf)))
            exp_g.append(jnp.exp(g_col))
            wq_ref[bi, ci, h, CHUNK:, :] = (q16.astype(F32) * exp_g[-1]).astype(BF16)
            ks_ref[bi, ci, h] = (k * jnp.exp(g_last - g_col)).astype(BF16)
            kq.append(_dot_nt(jnp.concatenate([kb[-1].astype(BF16), q16], axis=0), k16))
            if len(kq) % DN_HEADS == 0:
                yield
        def blockdiag(x):
            x16 = x.astype(BF16)
            zero = jnp.zeros_like(x16)
            return jnp.concatenate([jnp.where(left, x16, zero), jnp.where(left, zero, x16)], axis=0)

        neg_a = [-jnp.where(strict, x[0:CHUNK] * d, 0.0) for x, d in zip(kq, decay)]
        p = [jnp.concatenate([neg_a[i], neg_a[i + 1]], axis=1) for i in range(0, len(probs), 2)]
        inv = [eye2 + x for x in p]
        size = 2
        while size < CHUNK:
            p = [_dot(x.astype(BF16), blockdiag(x)) for x in p]
            yield
            inv = [t + _dot(t.astype(BF16), blockdiag(x)) for t, x in zip(inv, p)]
            yield
            size *= 2
        for i, (bi, ci, h) in enumerate(probs):
            rows = slice(ci * CHUNK, (ci + 1) * CHUNK)
            beta = beta_all[bi][rows, LANE_BETA + h:LANE_BETA + h + 1]
            vb = v_ref[bi, rows, h * DN_DV:(h + 1) * DN_DV].astype(F32) * beta
            rhs = jnp.concatenate([vb, kb[i] * exp_g[i]], axis=1).astype(BF16)
            pad = jnp.zeros_like(rhs)
            rhs2 = jnp.concatenate([rhs, pad] if i % 2 == 0 else [pad, rhs], axis=0)
            uw = _dot(inv[i // 2].astype(BF16), rhs2)
            u_ref[bi, ci, h] = uw[:, 0:DN_DV]
            wq_ref[bi, ci, h, 0:CHUNK, :] = uw[:, DN_DV:].astype(BF16)
            qk_ref[bi, ci, h] = (kq[i][CHUNK:] * decay[i]).astype(BF16)
        yield

    def phase_b(chunks):
        seqs = [(bi, h) for bi in range(nb) for h in heads]
        for ci in chunks:
            rows = slice(ci * CHUNK, (ci + 1) * CHUNK)
            last = ci * CHUNK + CHUNK - 1
            s = [s_ref[bi, h] for bi, h in seqs]
            ws = []
            for i, ((bi, h), x) in enumerate(zip(seqs, s)):
                ws.append(_dot(wq_ref[bi, ci, h], x.astype(BF16)))
                if i % GDN_YIELD == GDN_YIELD - 1:
                    yield
            vn16, o = [], []
            for i, ((bi, h), x, y) in enumerate(zip(seqs, s, ws)):
                vn = (u_ref[bi, ci, h] - y[0:CHUNK]).astype(BF16)
                vn16.append(vn)
                o.append(y[CHUNK:] + _dot(qk_ref[bi, ci, h], vn))
                g_chunk = jnp.exp(g_row_all[bi][LANE_DA + h:LANE_DA + h + 1, last:last + 1])
                s_ref[bi, h] = x * g_chunk + _dot_tn(ks_ref[bi, ci, h], vn)
                if i % GDN_YIELD == GDN_YIELD - 1:
                    yield
            for i, ((bi, h), x) in enumerate(zip(seqs, o)):
                cols = slice(h * DN_DV, (h + 1) * DN_DV)
                on = x * lax.rsqrt(jnp.mean(x * x, axis=-1, keepdims=True) + EPS) * hnw
                o_ref[bi, rows, cols] = (on * gate_ref[bi, rows, cols].astype(F32)).astype(o_ref.dtype)
                if i % GDN_YIELD == GDN_YIELD - 1:
                    yield

    pa = phase_a(range(nc))
    _interleave(itertools.islice(pa, nb * nc), *side_0)
    _interleave(pa, *side_a)
    _interleave(phase_b(range(nc)), *side_b)


def _gdn(plain, small, small_t, gates, par, par_t, hnw, b, t):
    tt = min(GDN_TT, t)
    nb = GDN_NB if b % GDN_NB == 0 else 1
    nc = tt // CHUNK
    return pl.pallas_call(
        _gdn_body,
        grid=(b // nb, t // tt),
        in_specs=[
            pl.BlockSpec((nb, tt, DN_QK), lambda bi, ti: (bi, ti, 2)),
            pl.BlockSpec((nb, tt, DN_QK), lambda bi, ti: (bi, ti, 3)),
            pl.BlockSpec((nb, tt, DN_V), lambda bi, ti: (bi, ti, 4)),
            pl.BlockSpec((nb, tt, SMALL_W), lambda bi, ti: (bi, ti, 0)),
            pl.BlockSpec((nb, SMALL_T_ROWS, tt), lambda bi, ti: (bi, 0, ti)),
            pl.BlockSpec((nb, tt, DN_V), lambda bi, ti: (bi, ti, 1)),
            pl.BlockSpec((2, SMALL_W), lambda bi, ti: (0, 0)),
            pl.BlockSpec((SMALL_T_ROWS, 2), lambda bi, ti: (0, 0)),
            pl.BlockSpec((1, DN_DV), lambda bi, ti: (0, 0)),
        ],
        out_specs=pl.BlockSpec((nb, tt, DN_V), lambda bi, ti: (bi, ti, 0)),
        out_shape=jax.ShapeDtypeStruct((b, t, DN_V), BF16),
        scratch_shapes=[
            pltpu.VMEM((nb, DN_HEADS, DN_DK, DN_DV), F32),
            pltpu.VMEM((nb, nc, DN_HEADS, CHUNK, DN_DV), F32),
            pltpu.VMEM((nb, nc, DN_HEADS, 2 * CHUNK, DN_DK), BF16),
            pltpu.VMEM((nb, nc, DN_HEADS, CHUNK, CHUNK), BF16),
            pltpu.VMEM((nb, nc, DN_HEADS, CHUNK, DN_DK), BF16),
        ],
        compiler_params=_cparams(("parallel", "arbitrary")),
        name="gdn",
    )(plain, plain, plain, small, small_t, gates, par, par_t, hnw)


def _mixers_body(*refs):
    (gla_qk, gla_v, gla_sm, gla_gate, wgg, bgg, gla_hnw,
     dn_q, dn_k, dn_v, dn_sm, dn_smt, dn_gate, par, part, dn_hnw,
     ya_ref, yb_ref, gla_st, dn_s, dn_u, dn_wq, dn_qk, dn_ks) = refs
    @pl.when(pl.program_id(1) == 0)
    def _():
        gla_st[...] = jnp.zeros_like(gla_st)

    gla = _gla_stream(gla_qk, gla_v, gla_sm, gla_gate, wgg, bgg, gla_hnw, ya_ref, gla_st)
    _gdn_body(dn_q, dn_k, dn_v, dn_sm, dn_smt, dn_gate, par, part, dn_hnw, yb_ref,
              dn_s, dn_u, dn_wq, dn_qk, dn_ks, side_b=(gla,))


def _mixers(plain, small, small_t, gates, wgg, bgg, gla_hnw, par, par_t, dn_hnw, b, t):
    tt = min(MIX_TT, t)
    nc = tt // CHUNK
    blk = lambda w, j: pl.BlockSpec((1, tt, w), lambda bi, ti: (bi, ti, j))
    const = lambda shape: pl.BlockSpec(shape, lambda bi, ti: (0,) * len(shape))
    return pl.pallas_call(
        _mixers_body,
        grid=(b, t // tt),
        in_specs=[
            blk(2 * GLA_QK, 0), blk(GLA_V, 1), blk(SMALL_W, 0), blk(GLA_V, 0),
            const((SMALL_W, GLA_QK)), const((1, GLA_QK)), const((1, GLA_DV)),
            blk(DN_QK, 2), blk(DN_QK, 3), blk(DN_V, 4), blk(SMALL_W, 0),
            pl.BlockSpec((1, SMALL_T_ROWS, tt), lambda bi, ti: (bi, 0, ti)),
            blk(DN_V, 1),
            const((2, SMALL_W)), const((SMALL_T_ROWS, 2)), const((1, DN_DV)),
        ],
        out_specs=[blk(GLA_V, 0), blk(DN_V, 0)],
        out_shape=[jax.ShapeDtypeStruct((b, t, GLA_V), BF16), jax.ShapeDtypeStruct((b, t, DN_V), BF16)],
        scratch_shapes=[
            pltpu.VMEM((GLA_HEADS, GLA_DV, GLA_DK), F32),
            pltpu.VMEM((1, DN_HEADS, DN_DK, DN_DV), F32),
            pltpu.VMEM((1, nc, DN_HEADS, CHUNK, DN_DV), F32),
            pltpu.VMEM((1, nc, DN_HEADS, 2 * CHUNK, DN_DK), BF16),
            pltpu.VMEM((1, nc, DN_HEADS, CHUNK, CHUNK), BF16),
            pltpu.VMEM((1, nc, DN_HEADS, CHUNK, DN_DK), BF16),
        ],
        compiler_params=_cparams(("parallel", "arbitrary")),
        name="mixers",
    )(plain, plain, small, gates, wgg, bgg, gla_hnw,
      plain, plain, plain, small, small_t, gates, par, par_t, dn_hnw)


def _layer(h, ffn1_norm, ffn1_w_gate, ffn1_w_up, ffn1_w_down, mix_norm, w_in, w_gla_gate, b_gla_gate,
           conv_w, dn_a_log, dn_dt_bias, gla_head_norm, dn_head_norm, w_out, ffn2_norm, ffn2_w_gate,
           ffn2_w_up, ffn2_w_down, final_w, final_norm):
    b, t, d = h.shape
    m = b * t
    row = lambda a: a.reshape(1, -1).astype(F32)

    h1 = _ffn(h.reshape(m, d), row(ffn1_norm), ffn1_w_gate.astype(BF16), ffn1_w_up.astype(BF16),
              ffn1_w_down.astype(BF16))

    sizes = (GLA_QK, GLA_QK, GLA_V, GLA_V, GLA_RANK, DN_QK, DN_QK, DN_V, DN_V, DN_HEADS, DN_HEADS,
             D_MODEL, D_MODEL)
    offs = [0]
    for s in sizes:
        offs.append(offs[-1] + s)
    col = lambda i: w_in[:, offs[i]:offs[i + 1]]
    (w_gq, w_gk, w_gv, w_gr, w_glr, w_dq, w_dk, w_dv, w_dgate, w_dbeta, w_da, w_ma, w_mb) = (
        col(i) for i in range(len(sizes)))
    w_plain = jnp.concatenate([w_gq, w_gk, w_gv, w_dq, w_dk, w_dv], axis=1).astype(BF16)
    w_gate_a = jnp.concatenate([w_gr, w_dgate], axis=1).astype(BF16)
    w_gate_b = jnp.concatenate([w_ma, w_mb], axis=1).astype(BF16)
    w_small = jnp.concatenate(
        [w_glr, w_dbeta, w_da, jnp.zeros((d, SMALL_W - LANE_DA - DN_HEADS), F32)], axis=1).astype(BF16)
    w_small_t = w_small[:, :SMALL_T_ROWS].T

    mixn = row(mix_norm)
    plain, gates, small, small_t = _proj(h1.reshape(b, t, d), mixn, w_plain, w_gate_a, w_gate_b, w_small,
                                         w_small_t, conv_w.astype(F32))
    plain, gates, small = (a.reshape(b, t, -1) for a in (plain, gates, small))

    wgg = jnp.zeros((SMALL_W, GLA_QK), F32).at[LANE_GLR:LANE_GLR + GLA_RANK].set(w_gla_gate).astype(BF16)
    par = jnp.zeros((2, SMALL_W), F32)
    par = par.at[0, LANE_DA:LANE_DA + DN_HEADS].set(dn_a_log.astype(F32))
    par = par.at[1, LANE_DA:LANE_DA + DN_HEADS].set(dn_dt_bias.astype(F32))
    par_t = par[:, :SMALL_T_ROWS].T
    ya, yb = _mixers(plain, small, small_t, gates, wgg, row(b_gla_gate), row(gla_head_norm), par, par_t,
                     row(dn_head_norm), b, t)

    h3 = _mix_out_ffn(h1, ya.reshape(m, d), yb.reshape(m, d), w_out.astype(BF16), row(ffn2_norm),
                      ffn2_w_gate.astype(BF16), ffn2_w_up.astype(BF16), ffn2_w_down.astype(BF16),
                      row(final_w), final_norm)
    return h3.reshape(b, t, d)


def kernel(x, ffn1_norm, ffn1_w_gate, ffn1_w_up, ffn1_w_down, mix_norm, w_in, w_gla_gate, b_gla_gate,
           conv_w, dn_a_log, dn_dt_bias, gla_head_norm, dn_head_norm, w_out, ffn2_norm, ffn2_w_gate,
           ffn2_w_up, ffn2_w_down, final_norm):
    depth = ffn1_norm.shape[0]
    h = x
    for layer in range(depth):
        h = _layer(h, ffn1_norm[layer], ffn1_w_gate[layer], ffn1_w_up[layer], ffn1_w_down[layer],
                   mix_norm[layer], w_in[layer], w_gla_gate[layer], b_gla_gate[layer], conv_w[layer],
                   dn_a_log[layer], dn_dt_bias[layer], gla_head_norm[layer], dn_head_norm[layer],
                   w_out[layer], ffn2_norm[layer], ffn2_w_gate[layer], ffn2_w_up[layer],
                   ffn2_w_down[layer], final_norm, layer == depth - 1)
    return h
```

```python
import functools
import itertools

import jax
import jax.numpy as jnp
from jax import lax
from jax.experimental import pallas as pl
from jax.experimental.pallas import tpu as pltpu

F32 = jnp.float32
BF16 = jnp.bfloat16

D_MODEL = 1024
D_FF = 2816
FFN_RES = 0.5
GLA_HEADS = 4
GLA_DK = 128
GLA_DV = 256
GLA_RANK = 16
GLA_TAU = 16.0
DN_HEADS = 8
DN_DK = 128
DN_DV = 128
CONV_K = 4
CHUNK = 64
EPS = 1e-6

GLA_QK = GLA_HEADS * GLA_DK
GLA_V = GLA_HEADS * GLA_DV
DN_QK = DN_HEADS * DN_DK
DN_V = DN_HEADS * DN_DV

SMALL_W = 128
SMALL_T_ROWS = 32
LANE_GLR = 0
LANE_BETA = GLA_RANK
LANE_DA = GLA_RANK + DN_HEADS

FFN_TM = 512
FFN_TF = 256
PROJ_TM = 512
PROJ_TN = 256
CONV_TN = 256
CONV_STRIP = 128
CONV_PIECE = 1
CONV_SLOTS = 4
CONV_PAD = 128
MIX_TT = 256
GDN_TT = 256
GDN_NB = 1
GDN_YIELD = 8
CONV_HALO = 8

VMEM_LIMIT_BYTES = 48 * 1024 * 1024


def _cparams(sem, flags=None):
    return pltpu.CompilerParams(dimension_semantics=sem, vmem_limit_bytes=VMEM_LIMIT_BYTES, flags=flags)


def _rms(x, w):
    return x * lax.rsqrt(jnp.mean(x * x, axis=-1, keepdims=True) + EPS) * w


def _sigmoid(x):
    return jax.nn.sigmoid(x)


def _silu(x):
    return x * _sigmoid(x)


def _softplus(x):
    return jnp.maximum(x, 0.0) + jnp.log1p(jnp.exp(-jnp.abs(x)))


def _dot(a, b):
    return jnp.dot(a, b, preferred_element_type=F32)


def _dot_nt(a, b):
    return lax.dot_general(a, b, (((1,), (1,)), ((), ())), preferred_element_type=F32)


def _dot_tn(a, b):
    return lax.dot_general(a, b, (((0,), (0,)), ((), ())), preferred_element_type=F32)


def _split3(x):
    hi = x.astype(BF16)
    r1 = x - hi.astype(F32)
    mid = r1.astype(BF16)
    lo = (r1 - mid.astype(F32)).astype(BF16)
    return hi, mid, lo


def _ones_dot_rhs(ones, x):
    n = x.shape[1]
    y = _dot(ones.astype(BF16), jnp.concatenate(_split3(x), axis=1))
    return y[:, 0:n] + y[:, n:2 * n] + y[:, 2 * n:]


def _ones_dot_lhs(x, ones):
    m = x.shape[0]
    y = _dot(jnp.concatenate(_split3(x), axis=0), ones.astype(BF16))
    return y[0:m] + y[m:2 * m] + y[2 * m:]


def _swiglu_half_step(x, nw_ref, wg_ref, wu_ref, wd_ref):
    xn = _rms(x, nw_ref[...]).astype(BF16)
    ff = wg_ref.shape[1]
    acc = None
    for f0 in range(0, ff, FFN_TF):
        g = _dot(xn, wg_ref[:, f0:f0 + FFN_TF])
        u = _dot(xn, wu_ref[:, f0:f0 + FFN_TF])
        part = _dot((_silu(g) * u).astype(BF16), wd_ref[f0:f0 + FFN_TF, :])
        acc = part if acc is None else acc + part
    return x + FFN_RES * acc


def _ffn_body(x_ref, nw_ref, wg_ref, wu_ref, wd_ref, o_ref):
    o_ref[...] = _swiglu_half_step(x_ref[...], nw_ref, wg_ref, wu_ref, wd_ref)


def _mix_out_ffn_body(h_ref, ya_ref, yb_ref, wo_ref, nw_ref, wg_ref, wu_ref, wd_ref, fw_ref, o_ref, *,
                      final_norm):
    y = (ya_ref[...].astype(F32) + yb_ref[...].astype(F32)).astype(BF16)
    h = _swiglu_half_step(h_ref[...] + _dot(y, wo_ref[...]), nw_ref, wg_ref, wu_ref, wd_ref)
    if final_norm:
        h = _rms(h, fw_ref[...])
    o_ref[...] = h


def _resident(shape):
    return pl.BlockSpec(shape, lambda *_: (0,) * len(shape), pipeline_mode=pl.Buffered(1))


def _ffn(x2d, norm_w, wg, wu, wd):
    m, d = x2d.shape
    ff = wg.shape[1]
    tm = min(FFN_TM, m)
    row = pl.BlockSpec((tm, d), lambda i: (i, 0))
    return pl.pallas_call(
        _ffn_body,
        grid=(m // tm,),
        in_specs=[row, _resident((1, d)), _resident((d, ff)), _resident((d, ff)), _resident((ff, d))],
        out_specs=row,
        out_shape=jax.ShapeDtypeStruct((m, d), F32),
        compiler_params=_cparams(("parallel",)),
        name="ffn",
    )(x2d, norm_w, wg, wu, wd)


def _mix_out_ffn(h2d, ya, yb, wo, norm_w, wg, wu, wd, final_w, final_norm):
    m, d = h2d.shape
    ff = wg.shape[1]
    tm = min(FFN_TM, m)
    row = pl.BlockSpec((tm, d), lambda i: (i, 0))
    return pl.pallas_call(
        functools.partial(_mix_out_ffn_body, final_norm=final_norm),
        grid=(m // tm,),
        in_specs=[row, row, row, _resident((d, d)), _resident((1, d)), _resident((d, ff)),
                  _resident((d, ff)), _resident((ff, d)), _resident((1, d))],
        out_specs=row,
        out_shape=jax.ShapeDtypeStruct((m, d), F32),
        compiler_params=_cparams(("parallel",)),
        name="mix_out_ffn",
    )(h2d, ya, yb, wo, norm_w, wg, wu, wd, final_w)


def _proj_body(x_ref, nw_ref, w_ref, wst_ref, cw_ref, plain_ref, gate_ref, sm_ref, smt_ref, carry_ref, pe_ref):
    tm = x_ref.shape[0]
    n_plain, n_gate = plain_ref.shape[1], gate_ref.shape[1]
    wp_ref = w_ref.at[:, 0:n_plain]
    wa_ref = w_ref.at[:, n_plain:n_plain + n_gate]
    wb_ref = w_ref.at[:, n_plain + n_gate:n_plain + 2 * n_gate]
    ws_ref = w_ref.at[:, n_plain + 2 * n_gate:n_plain + 2 * n_gate + SMALL_W]
    n_gla = 2 * GLA_QK + GLA_V
    n_dn_qk = 2 * DN_QK

    @pl.when(pl.program_id(1) == 0)
    def _():
        carry_ref[...] = jnp.zeros_like(carry_ref)

    xn = _rms(x_ref[...], nw_ref[...]).astype(BF16)

    def plain_chunk(n0):
        cols = slice(n0, n0 + PROJ_TN)
        plain_ref[:, cols] = _dot(xn, wp_ref[:, cols]).astype(plain_ref.dtype)

    def gate_chunk(n0):
        cols = slice(n0, n0 + PROJ_TN)
        gate_ref[:, cols] = (_silu(_dot(xn, wa_ref[:, cols])) *
                             _sigmoid(_dot(xn, wb_ref[:, cols]))).astype(gate_ref.dtype)

    def conv_dot(ic):
        c0 = ic * CONV_TN
        ccols = slice(c0, c0 + CONV_TN)
        pe = pe_ref.at[ic % pe_ref.shape[0]]
        pe[0:CONV_HALO, 0:CONV_TN] = carry_ref[:, ccols]
        pe[CONV_HALO:, 0:CONV_TN] = _dot(xn, wp_ref[:, n_gla + c0:n_gla + c0 + CONV_TN])
        carry_ref[:, ccols] = pe[tm:tm + CONV_HALO, 0:CONV_TN]

    def conv_strips(ic):
        c0 = ic * CONV_TN
        ccols = slice(c0, c0 + CONV_TN)
        pe = pe_ref.at[ic % pe_ref.shape[0]]
        for r0 in range(0, tm, CONV_STRIP):
            win = pe[r0:r0 + CONV_STRIP + CONV_HALO, 0:CONV_TN]
            w0, w1, w2, w3 = (cw_ref[j:j + 1, ccols] for j in range(CONV_K))
            prev = pltpu.roll(win, 1, axis=0)
            y = (w3 * win + w2 * prev) + pltpu.roll(w1 * win + w0 * prev, 2, axis=0)
            y = _silu(y[CONV_HALO:, :])
            if c0 < n_dn_qk:
                scale = DN_DK ** -0.5 if c0 < DN_QK else 1.0
                heads = [y[:, h0:h0 + DN_DK] for h0 in range(0, CONV_TN, DN_DK)]
                y = jnp.concatenate(
                    [yh * (lax.rsqrt(jnp.sum(yh * yh, axis=-1, keepdims=True) + EPS) * scale)
                     for yh in heads], axis=1)
            plain_ref[r0:r0 + CONV_STRIP, n_gla + c0:n_gla + c0 + CONV_TN] = y.astype(plain_ref.dtype)
            if (r0 // CONV_STRIP) % CONV_PIECE == CONV_PIECE - 1:
                yield

    def mxu_pieces(tasks):
        for task in tasks:
            task()
            yield

    plain_tasks = [functools.partial(plain_chunk, n0) for n0 in range(0, n_gla, PROJ_TN)]
    gate_tasks = [functools.partial(gate_chunk, n0) for n0 in range(0, wa_ref.shape[1], PROJ_TN)]
    others = [t for pair in itertools.zip_longest(plain_tasks, gate_tasks) for t in pair if t is not None]
    n_conv = (wp_ref.shape[1] - n_gla) // CONV_TN
    conv_dot(0)
    for ic in range(n_conv):
        lo, hi = ic * len(others) // n_conv, (ic + 1) * len(others) // n_conv
        if ic + 1 < n_conv:
            conv_dot(ic + 1)
        _interleave(conv_strips(ic), mxu_pieces(others[lo:hi]))
    sm_ref[...] = _dot(xn, ws_ref[...])
    smt_ref[0] = _dot_nt(wst_ref[...], xn)


def _proj(h3d, norm_w, w_all, n_plain, n_gate, w_small_t, conv_w):
    b, t, d = h3d.shape
    tm = min(PROJ_TM, t)
    nt = t // tm
    n_conv = conv_w.shape[1]
    assert w_all.shape[1] == n_plain + 2 * n_gate + SMALL_W
    rows = lambda n: pl.BlockSpec((tm, n), lambda bi, ti: (bi * nt + ti, 0))
    return pl.pallas_call(
        _proj_body,
        grid=(b, nt),
        in_specs=[rows(d), _resident((1, d)), _resident(w_all.shape), _resident((SMALL_T_ROWS, d)),
                  _resident((CONV_K, n_conv))],
        out_specs=[rows(n_plain), rows(n_gate), rows(SMALL_W),
                   pl.BlockSpec((1, SMALL_T_ROWS, tm), lambda bi, ti: (bi, 0, ti))],
        out_shape=[
            jax.ShapeDtypeStruct((b * t, n_plain), BF16),
            jax.ShapeDtypeStruct((b * t, n_gate), BF16),
            jax.ShapeDtypeStruct((b * t, SMALL_W), F32),
            jax.ShapeDtypeStruct((b, SMALL_T_ROWS, t), F32),
        ],
        scratch_shapes=[pltpu.VMEM((CONV_HALO, n_conv), F32),
                        pltpu.VMEM((CONV_SLOTS, CONV_HALO + tm, CONV_TN + CONV_PAD), F32)],
        compiler_params=_cparams(("parallel", "arbitrary")),
        name="proj",
    )(h3d.reshape(b * t, d), norm_w, w_all, w_small_t, conv_w)


def _tri_masks():
    r = lax.broadcasted_iota(jnp.int32, (CHUNK, CHUNK), 0)
    c = lax.broadcasted_iota(jnp.int32, (CHUNK, CHUNK), 1)
    return r >= c, r > c, r <= c


def _gla_stream(qk_ref, v_ref, sm_ref, gate_ref, wgg_ref, bgg_ref, hnw_ref, o_ref, st_ref):
    causal, _, _ = _tri_masks()
    tt = qk_ref.shape[1]
    nc = tt // CHUNK
    r = lax.broadcasted_iota(jnp.int32, (tt, tt), 0)
    c = lax.broadcasted_iota(jnp.int32, (tt, tt), 1)
    bd_tril = jnp.where(((r // CHUNK) == (c // CHUNK)) & (r >= c), 1.0, 0.0).astype(F32)

    log_a = []
    for ci in range(nc):
        z = _dot(sm_ref[0, ci * CHUNK:(ci + 1) * CHUNK, :].astype(BF16), wgg_ref[...]) + bgg_ref[...]
        log_a.append((jnp.minimum(z, 0.0) - jnp.log1p(jnp.exp(-jnp.abs(z)))) * (1.0 / GLA_TAU))
        yield
    b = _ones_dot_rhs(bd_tril, jnp.concatenate(log_a, axis=0))
    hnw = hnw_ref[...]
    yield

    probs = [(ci, h) for ci in range(nc) for h in range(GLA_HEADS)]
    q_in, k_neg, k_state, a_chunk = [], [], [], []
    for ci in range(nc):
        rows = slice(ci * CHUNK, (ci + 1) * CHUNK)
        bc = b[rows]
        b_last = bc[CHUNK - 1:CHUNK, :]
        q = qk_ref[0, rows, 0:GLA_QK].astype(F32) * (GLA_DK ** -0.5)
        k = qk_ref[0, rows, GLA_QK:2 * GLA_QK].astype(F32)
        q_in.append((q * jnp.exp(bc)).astype(BF16))
        k_neg.append((k * jnp.exp(-bc)).astype(BF16))
        k_state.append((k * jnp.exp(b_last - bc)).astype(BF16))
        a_chunk.append(jnp.exp(b_last))
        yield
    kcols = lambda h: slice(h * GLA_DK, (h + 1) * GLA_DK)
    vcols = lambda h: slice(h * GLA_DV, (h + 1) * GLA_DV)
    vh = [v_ref[0, ci * CHUNK:(ci + 1) * CHUNK, vcols(h)] for ci, h in probs]
    scores, o_intra, d_st = [], [], []
    for v, (ci, h) in zip(vh, probs):
        s = _dot_nt(q_in[ci][:, kcols(h)], k_neg[ci][:, kcols(h)])
        scores.append(jnp.where(causal, s, 0.0).astype(BF16))
        d_st.append(_dot_tn(v, k_state[ci][:, kcols(h)]))
        if h % 2 == 1:
            yield
    for s, v, (ci, h) in zip(scores, vh, probs):
        o_intra.append(_dot(s, v))
        if h % 2 == 1:
            yield

    for i, (ci, h) in enumerate(probs):
        rows = slice(ci * CHUNK, (ci + 1) * CHUNK)
        st = st_ref[h]
        o = o_intra[i] + _dot_nt(q_in[ci][:, kcols(h)], st.astype(BF16))
        st_ref[h] = st * a_chunk[ci][:, kcols(h)] + d_st[i]
        o = o * lax.rsqrt(jnp.mean(o * o, axis=-1, keepdims=True) + EPS) * hnw
        o_ref[0, rows, vcols(h)] = (o * gate_ref[0, rows, vcols(h)].astype(F32)).astype(o_ref.dtype)
        if h % 2 == 1:
            yield


def _interleave(*streams):
    streams = list(streams)
    while streams:
        for g in list(streams):
            if next(g, StopIteration) is StopIteration:
                streams.remove(g)


def _gla_body(qk_ref, v_ref, sm_ref, gate_ref, wgg_ref, bgg_ref, hnw_ref, o_ref, st_ref):
    @pl.when(pl.program_id(1) == 0)
    def _():
        st_ref[...] = jnp.zeros_like(st_ref)

    _interleave(_gla_stream(qk_ref, v_ref, sm_ref, gate_ref, wgg_ref, bgg_ref, hnw_ref, o_ref, st_ref))


def _gla(plain, small, gates, wgg, bgg, hnw, b, t):
    tt = min(MIX_TT, t)
    return pl.pallas_call(
        _gla_body,
        grid=(b, t // tt),
        in_specs=[
            pl.BlockSpec((1, tt, 2 * GLA_QK), lambda bi, ti: (bi, ti, 0)),
            pl.BlockSpec((1, tt, GLA_V), lambda bi, ti: (bi, ti, 1)),
            pl.BlockSpec((1, tt, SMALL_W), lambda bi, ti: (bi, ti, 0)),
            pl.BlockSpec((1, tt, GLA_V), lambda bi, ti: (bi, ti, 0)),
            pl.BlockSpec((SMALL_W, GLA_QK), lambda bi, ti: (0, 0)),
            pl.BlockSpec((1, GLA_QK), lambda bi, ti: (0, 0)),
            pl.BlockSpec((1, GLA_DV), lambda bi, ti: (0, 0)),
        ],
        out_specs=pl.BlockSpec((1, tt, GLA_V), lambda bi, ti: (bi, ti, 0)),
        out_shape=jax.ShapeDtypeStruct((b, t, GLA_V), BF16),
        scratch_shapes=[pltpu.VMEM((GLA_HEADS, GLA_DV, GLA_DK), F32)],
        compiler_params=_cparams(("parallel", "arbitrary")),
        name="gla",
    )(plain, plain, small, gates, wgg, bgg, hnw)


def _gdn_body(q_ref, k_ref, v_ref, sm_ref, smt_ref, gate_ref, par_ref, part_ref, hnw_ref,
              o_ref, s_ref, u_ref, wq_ref, qk_ref, ks_ref, side_0=(), side_a=(), side_b=()):
    nb, tt = q_ref.shape[0], q_ref.shape[1]
    nc = tt // CHUNK
    heads = range(DN_HEADS)

    @pl.when(pl.program_id(1) == 0)
    def _():
        s_ref[...] = jnp.zeros_like(s_ref)

    causal, strict, _ = _tri_masks()
    r = lax.broadcasted_iota(jnp.int32, (tt, tt), 0)
    c = lax.broadcasted_iota(jnp.int32, (tt, tt), 1)
    same_chunk = (r // CHUNK) == (c // CHUNK)
    bd_tril = jnp.where(same_chunk & (r >= c), 1.0, 0.0).astype(F32)
    bd_triu = jnp.where(same_chunk & (r <= c), 1.0, 0.0).astype(F32)

    beta_all, g_col_all, g_row_all = [], [], []
    for bi in range(nb):
        sm = sm_ref[bi]
        beta_all.append(_sigmoid(sm))
        g_all = -jnp.exp(par_ref[0:1, :]) * _softplus(sm + par_ref[1:2, :])
        smt = smt_ref[bi]
        gt_all = -jnp.exp(part_ref[:, 0:1]) * _softplus(smt + part_ref[:, 1:2])
        g_col_all.append(_ones_dot_rhs(bd_tril, g_all))
        g_row_all.append(_ones_dot_lhs(gt_all, bd_triu))
    hnw = hnw_ref[...]

    ri = lax.broadcasted_iota(jnp.int32, (CHUNK, 2 * CHUNK), 0)
    li = lax.broadcasted_iota(jnp.int32, (CHUNK, 2 * CHUNK), 1)
    left = li < CHUNK
    eye2 = jnp.where((ri == li) | (ri + CHUNK == li), 1.0, 0.0).astype(F32)

    def phase_a(chunks):
        probs = [(bi, ci, h) for bi in range(nb) for ci in chunks for h in heads]
        kb, decay, exp_g, kq = [], [], [], []
        for bi, ci, h in probs:
            rows = slice(ci * CHUNK, (ci + 1) * CHUNK)
            q16 = q_ref[bi, rows, h * DN_DK:(h + 1) * DN_DK]
            k16 = k_ref[bi, rows, h * DN_DK:(h + 1) * DN_DK]
            k = k16.astype(F32)
            beta = beta_all[bi][rows, LANE_BETA + h:LANE_BETA + h + 1]
            g_col = g_col_all[bi][rows, LANE_DA + h:LANE_DA + h + 1]
            g_row = g_row_all[bi][LANE_DA + h:LANE_DA + h + 1, rows]
            g_last = g_row[:, CHUNK - 1:CHUNK]
            kb.append(k * beta)
            decay.append(jnp.exp(jnp.where(causal, g_col - g_row, -jnp.inf)))
            exp_g.append(jnp.exp(g_col))
            wq_ref[bi, ci, h, CHUNK:, :] = (q16.astype(F32) * exp_g[-1]).astype(BF16)
            ks_ref[bi, ci, h] = (k * jnp.exp(g_last - g_col)).astype(BF16)
            kq.append(_dot_nt(jnp.concatenate([kb[-1].astype(BF16), q16], axis=0), k16))
            if len(kq) % DN_HEADS == 0:
                yield
        def blockdiag(x):
            x16 = x.astype(BF16)
            zero = jnp.zeros_like(x16)
            return jnp.concatenate([jnp.where(left, x16, zero), jnp.where(left, zero, x16)], axis=0)

        neg_a = [-jnp.where(strict, x[0:CHUNK] * d, 0.0) for x, d in zip(kq, decay)]
        p = [jnp.concatenate([neg_a[i], neg_a[i + 1]], axis=1) for i in range(0, len(probs), 2)]
        inv = [eye2 + x for x in p]
        size = 2
        while size < CHUNK:
            p = [_dot(x.astype(BF16), blockdiag(x)) for x in p]
            yield
            inv = [t + _dot(t.astype(BF16), blockdiag(x)) for t, x in zip(inv, p)]
            yield
            size *= 2
        for i, (bi, ci, h) in enumerate(probs):
            rows = slice(ci * CHUNK, (ci + 1) * CHUNK)
            beta = beta_all[bi][rows, LANE_BETA + h:LANE_BETA + h + 1]
            vb = v_ref[bi, rows, h * DN_DV:(h + 1) * DN_DV].astype(F32) * beta
            rhs = jnp.concatenate([vb, kb[i] * exp_g[i]], axis=1).astype(BF16)
            pad = jnp.zeros_like(rhs)
            rhs2 = jnp.concatenate([rhs, pad] if i % 2 == 0 else [pad, rhs], axis=0)
            uw = _dot(inv[i // 2].astype(BF16), rhs2)
            u_ref[bi, ci, h] = uw[:, 0:DN_DV]
            wq_ref[bi, ci, h, 0:CHUNK, :] = uw[:, DN_DV:].astype(BF16)
            qk_ref[bi, ci, h] = (kq[i][CHUNK:] * decay[i]).astype(BF16)
        yield

    def phase_b(chunks):
        seqs = [(bi, h) for bi in range(nb) for h in heads]
        for ci in chunks:
            rows = slice(ci * CHUNK, (ci + 1) * CHUNK)
            last = ci * CHUNK + CHUNK - 1
            s = [s_ref[bi, h] for bi, h in seqs]
            ws = []
            for i, ((bi, h), x) in enumerate(zip(seqs, s)):
                ws.append(_dot(wq_ref[bi, ci, h], x.astype(BF16)))
                if i % GDN_YIELD == GDN_YIELD - 1:
                    yield
            vn16, o = [], []
            for i, ((bi, h), x, y) in enumerate(zip(seqs, s, ws)):
                vn = (u_ref[bi, ci, h] - y[0:CHUNK]).astype(BF16)
                vn16.append(vn)
                o.append(y[CHUNK:] + _dot(qk_ref[bi, ci, h], vn))
                g_chunk = jnp.exp(g_row_all[bi][LANE_DA + h:LANE_DA + h + 1, last:last + 1])
                s_ref[bi, h] = x * g_chunk + _dot_tn(ks_ref[bi, ci, h], vn)
                if i % GDN_YIELD == GDN_YIELD - 1:
                    yield
            for i, ((bi, h), x) in enumerate(zip(seqs, o)):
                cols = slice(h * DN_DV, (h + 1) * DN_DV)
                on = x * lax.rsqrt(jnp.mean(x * x, axis=-1, keepdims=True) + EPS) * hnw
                o_ref[bi, rows, cols] = (on * gate_ref[bi, rows, cols].astype(F32)).astype(o_ref.dtype)
                if i % GDN_YIELD == GDN_YIELD - 1:
                    yield

    pa = phase_a(range(nc))
    _interleave(itertools.islice(pa, nb * nc), *side_0)
    _interleave(pa, *side_a)
    _interleave(phase_b(range(nc)), *side_b)


def _gdn(plain, small, small_t, gates, par, par_t, hnw, b, t):
    tt = min(GDN_TT, t)
    nb = GDN_NB if b % GDN_NB == 0 else 1
    nc = tt // CHUNK
    return pl.pallas_call(
        _gdn_body,
        grid=(b // nb, t // tt),
        in_specs=[
            pl.BlockSpec((nb, tt, DN_QK), lambda bi, ti: (bi, ti, 2)),
            pl.BlockSpec((nb, tt, DN_QK), lambda bi, ti: (bi, ti, 3)),
            pl.BlockSpec((nb, tt, DN_V), lambda bi, ti: (bi, ti, 4)),
            pl.BlockSpec((nb, tt, SMALL_W), lambda bi, ti: (bi, ti, 0)),
            pl.BlockSpec((nb, SMALL_T_ROWS, tt), lambda bi, ti: (bi, 0, ti)),
            pl.BlockSpec((nb, tt, DN_V), lambda bi, ti: (bi, ti, 1)),
            pl.BlockSpec((2, SMALL_W), lambda bi, ti: (0, 0)),
            pl.BlockSpec((SMALL_T_ROWS, 2), lambda bi, ti: (0, 0)),
            pl.BlockSpec((1, DN_DV), lambda bi, ti: (0, 0)),
        ],
        out_specs=pl.BlockSpec((nb, tt, DN_V), lambda bi, ti: (bi, ti, 0)),
        out_shape=jax.ShapeDtypeStruct((b, t, DN_V), BF16),
        scratch_shapes=[
            pltpu.VMEM((nb, DN_HEADS, DN_DK, DN_DV), F32),
            pltpu.VMEM((nb, nc, DN_HEADS, CHUNK, DN_DV), F32),
            pltpu.VMEM((nb, nc, DN_HEADS, 2 * CHUNK, DN_DK), BF16),
            pltpu.VMEM((nb, nc, DN_HEADS, CHUNK, CHUNK), BF16),
            pltpu.VMEM((nb, nc, DN_HEADS, CHUNK, DN_DK), BF16),
        ],
        compiler_params=_cparams(("parallel", "arbitrary")),
        name="gdn",
    )(plain, plain, plain, small, small_t, gates, par, par_t, hnw)


def _mixers_body(*refs):
    (gla_qk, gla_v, gla_sm, gla_gate, wgg, bgg, gla_hnw,
     dn_q, dn_k, dn_v, dn_sm, dn_smt, dn_gate, par, part, dn_hnw,
     ya_ref, yb_ref, gla_st, dn_s, dn_u, dn_wq, dn_qk, dn_ks) = refs
    @pl.when(pl.program_id(1) == 0)
    def _():
        gla_st[...] = jnp.zeros_like(gla_st)

    gla = _gla_stream(gla_qk, gla_v, gla_sm, gla_gate, wgg, bgg, gla_hnw, ya_ref, gla_st)
    _gdn_body(dn_q, dn_k, dn_v, dn_sm, dn_smt, dn_gate, par, part, dn_hnw, yb_ref,
              dn_s, dn_u, dn_wq, dn_qk, dn_ks, side_b=(gla,))


def _mixers(plain, small, small_t, gates, wgg, bgg, gla_hnw, par, par_t, dn_hnw, b, t):
    tt = min(MIX_TT, t)
    nc = tt // CHUNK
    blk = lambda w, j: pl.BlockSpec((1, tt, w), lambda bi, ti: (bi, ti, j))
    const = lambda shape: pl.BlockSpec(shape, lambda bi, ti: (0,) * len(shape))
    return pl.pallas_call(
        _mixers_body,
        grid=(b, t // tt),
        in_specs=[
            blk(2 * GLA_QK, 0), blk(GLA_V, 1), blk(SMALL_W, 0), blk(GLA_V, 0),
            const((SMALL_W, GLA_QK)), const((1, GLA_QK)), const((1, GLA_DV)),
            blk(DN_QK, 2), blk(DN_QK, 3), blk(DN_V, 4), blk(SMALL_W, 0),
            pl.BlockSpec((1, SMALL_T_ROWS, tt), lambda bi, ti: (bi, 0, ti)),
            blk(DN_V, 1),
            const((2, SMALL_W)), const((SMALL_T_ROWS, 2)), const((1, DN_DV)),
        ],
        out_specs=[blk(GLA_V, 0), blk(DN_V, 0)],
        out_shape=[jax.ShapeDtypeStruct((b, t, GLA_V), BF16), jax.ShapeDtypeStruct((b, t, DN_V), BF16)],
        scratch_shapes=[
            pltpu.VMEM((GLA_HEADS, GLA_DV, GLA_DK), F32),
            pltpu.VMEM((1, DN_HEADS, DN_DK, DN_DV), F32),
            pltpu.VMEM((1, nc, DN_HEADS, CHUNK, DN_DV), F32),
            pltpu.VMEM((1, nc, DN_HEADS, 2 * CHUNK, DN_DK), BF16),
            pltpu.VMEM((1, nc, DN_HEADS, CHUNK, CHUNK), BF16),
            pltpu.VMEM((1, nc, DN_HEADS, CHUNK, DN_DK), BF16),
        ],
        compiler_params=_cparams(("parallel", "arbitrary")),
        name="mixers",
    )(plain, plain, small, gates, wgg, bgg, gla_hnw,
      plain, plain, plain, small, small_t, gates, par, par_t, dn_hnw)


def _layer(h, ffn1_norm, ffn1_w_gate, ffn1_w_up, ffn1_w_down, mix_norm, w_in, w_gla_gate, b_gla_gate,
           conv_w, dn_a_log, dn_dt_bias, gla_head_norm, dn_head_norm, w_out, ffn2_norm, ffn2_w_gate,
           ffn2_w_up, ffn2_w_down, final_w, final_norm):
    b, t, d = h.shape
    m = b * t
    row = lambda a: a.reshape(1, -1).astype(F32)

    h1 = _ffn(h.reshape(m, d), row(ffn1_norm), ffn1_w_gate.astype(BF16), ffn1_w_up.astype(BF16),
              ffn1_w_down.astype(BF16))

    sizes = (GLA_QK, GLA_QK, GLA_V, GLA_V, GLA_RANK, DN_QK, DN_QK, DN_V, DN_V, DN_HEADS, DN_HEADS,
             D_MODEL, D_MODEL)
    offs = [0]
    for s in sizes:
        offs.append(offs[-1] + s)
    col = lambda i: w_in[:, offs[i]:offs[i + 1]]
    (w_gq, w_gk, w_gv, w_gr, w_glr, w_dq, w_dk, w_dv, w_dgate, w_dbeta, w_da, w_ma, w_mb) = (
        col(i) for i in range(len(sizes)))
    n_plain, n_gate = 2 * GLA_QK + GLA_V + 2 * DN_QK + DN_V, GLA_V + DN_V
    w_all = jnp.concatenate(
        [w_gq, w_gk, w_gv, w_dq, w_dk, w_dv, w_gr, w_dgate, w_ma, w_mb, w_glr, w_dbeta, w_da,
         jnp.zeros((d, SMALL_W - LANE_DA - DN_HEADS), F32)], axis=1).astype(BF16)
    w_small_t = w_all[:, n_plain + 2 * n_gate:n_plain + 2 * n_gate + SMALL_T_ROWS].T

    mixn = row(mix_norm)
    plain, gates, small, small_t = _proj(h1.reshape(b, t, d), mixn, w_all, n_plain, n_gate, w_small_t,
                                         conv_w.astype(F32))
    plain, gates, small = (a.reshape(b, t, -1) for a in (plain, gates, small))

    wgg = jnp.zeros((SMALL_W, GLA_QK), F32).at[LANE_GLR:LANE_GLR + GLA_RANK].set(w_gla_gate).astype(BF16)
    par = jnp.zeros((2, SMALL_W), F32)
    par = par.at[0, LANE_DA:LANE_DA + DN_HEADS].set(dn_a_log.astype(F32))
    par = par.at[1, LANE_DA:LANE_DA + DN_HEADS].set(dn_dt_bias.astype(F32))
    par_t = par[:, :SMALL_T_ROWS].T
    ya, yb = _mixers(plain, small, small_t, gates, wgg, row(b_gla_gate), row(gla_head_norm), par, par_t,
                     row(dn_head_norm), b, t)

    h3 = _mix_out_ffn(h1, ya.reshape(m, d), yb.reshape(m, d), w_out.astype(BF16), row(ffn2_norm),
                      ffn2_w_gate.astype(BF16), ffn2_w_up.astype(BF16), ffn2_w_down.astype(BF16),
                      row(final_w), final_norm)
    return h3.reshape(b, t, d)


def kernel(x, ffn1_norm, ffn1_w_gate, ffn1_w_up, ffn1_w_down, mix_norm, w_in, w_gla_gate, b_gla_gate,
           conv_w, dn_a_log, dn_dt_bias, gla_head_norm, dn_head_norm, w_out, ffn2_norm, ffn2_w_gate,
           ffn2_w_up, ffn2_w_down, final_norm):
    depth = ffn1_norm.shape[0]
    h = x
    for layer in range(depth):
        h = _layer(h, ffn1_norm[layer], ffn1_w_gate[layer], ffn1_w_up[layer], ffn1_w_down[layer],
                   mix_norm[layer], w_in[layer], w_gla_gate[layer], b_gla_gate[layer], conv_w[layer],
                   dn_a_log[layer], dn_dt_bias[layer], gla_head_norm[layer], dn_head_norm[layer],
                   w_out[layer], ffn2_norm[layer], ffn2_w_gate[layer], ffn2_w_up[layer],
                   ffn2_w_down[layer], final_norm, layer == depth - 1)
    return h
```

```python
import functools
import itertools

import jax
import jax.numpy as jnp
from jax import lax
from jax.experimental import pallas as pl
from jax.experimental.pallas import tpu as pltpu

F32 = jnp.float32
BF16 = jnp.bfloat16

D_MODEL = 1024
D_FF = 2816
FFN_RES = 0.5
GLA_HEADS = 4
GLA_DK = 128
GLA_DV = 256
GLA_RANK = 16
GLA_TAU = 16.0
DN_HEADS = 8
DN_DK = 128
DN_DV = 128
CONV_K = 4
CHUNK = 64
EPS = 1e-6

GLA_QK = GLA_HEADS * GLA_DK
GLA_V = GLA_HEADS * GLA_DV
DN_QK = DN_HEADS * DN_DK
DN_V = DN_HEADS * DN_DV

SMALL_W = 128
SMALL_T_ROWS = 32
LANE_GLR = 0
LANE_BETA = GLA_RANK
LANE_DA = GLA_RANK + DN_HEADS

FFN_TM = 512
FFN_TF = 256
PROJ_TM = 512
PROJ_TN = 256
CONV_TN = 256
CONV_STRIP = 128
CONV_PIECE = 1
CONV_SLOTS = 4
CONV_PAD = 128
MIX_TT = 256
GDN_TT = 256
GDN_NB = 1
GDN_YIELD = 8
CONV_HALO = 8

VMEM_LIMIT_BYTES = 48 * 1024 * 1024


def _cparams(sem, flags=None, fuse_inputs=None):
    return pltpu.CompilerParams(dimension_semantics=sem, vmem_limit_bytes=VMEM_LIMIT_BYTES, flags=flags,
                                allow_input_fusion=fuse_inputs)


def _rms(x, w):
    return x * lax.rsqrt(jnp.mean(x * x, axis=-1, keepdims=True) + EPS) * w


def _sigmoid(x):
    return jax.nn.sigmoid(x)


def _silu(x):
    return x * _sigmoid(x)


def _softplus(x):
    return jnp.maximum(x, 0.0) + jnp.log1p(jnp.exp(-jnp.abs(x)))


def _dot(a, b):
    return jnp.dot(a, b, preferred_element_type=F32)


def _dot_nt(a, b):
    return lax.dot_general(a, b, (((1,), (1,)), ((), ())), preferred_element_type=F32)


def _dot_tn(a, b):
    return lax.dot_general(a, b, (((0,), (0,)), ((), ())), preferred_element_type=F32)


def _split3(x):
    hi = x.astype(BF16)
    r1 = x - hi.astype(F32)
    mid = r1.astype(BF16)
    lo = (r1 - mid.astype(F32)).astype(BF16)
    return hi, mid, lo


def _ones_dot_rhs(ones, x):
    n = x.shape[1]
    y = _dot(ones.astype(BF16), jnp.concatenate(_split3(x), axis=1))
    return y[:, 0:n] + y[:, n:2 * n] + y[:, 2 * n:]


def _ones_dot_lhs(x, ones):
    m = x.shape[0]
    y = _dot(jnp.concatenate(_split3(x), axis=0), ones.astype(BF16))
    return y[0:m] + y[m:2 * m] + y[2 * m:]


def _swiglu_half_step(x, nw_ref, wg_ref, wu_ref, wd_ref):
    xn = _rms(x, nw_ref[...]).astype(BF16)
    ff = wg_ref.shape[1]
    acc = None
    for f0 in range(0, ff, FFN_TF):
        g = _dot(xn, wg_ref[:, f0:f0 + FFN_TF])
        u = _dot(xn, wu_ref[:, f0:f0 + FFN_TF])
        part = _dot((_silu(g) * u).astype(BF16), wd_ref[f0:f0 + FFN_TF, :])
        acc = part if acc is None else acc + part
    return x + FFN_RES * acc


def _ffn_body(x_ref, nw_ref, wg_ref, wu_ref, wd_ref, o_ref):
    o_ref[...] = _swiglu_half_step(x_ref[...], nw_ref, wg_ref, wu_ref, wd_ref)


def _mix_out_ffn_body(h_ref, ya_ref, yb_ref, wo_ref, nw_ref, wg_ref, wu_ref, wd_ref, fw_ref, o_ref, *,
                      final_norm):
    y = (ya_ref[...].astype(F32) + yb_ref[...].astype(F32)).astype(BF16)
    h = _swiglu_half_step(h_ref[...] + _dot(y, wo_ref[...]), nw_ref, wg_ref, wu_ref, wd_ref)
    if final_norm:
        h = _rms(h, fw_ref[...])
    o_ref[...] = h


def _resident(shape):
    return pl.BlockSpec(shape, lambda *_: (0,) * len(shape), pipeline_mode=pl.Buffered(1))


def _ffn(x2d, norm_w, wg, wu, wd):
    m, d = x2d.shape
    ff = wg.shape[1]
    tm = min(FFN_TM, m)
    row = pl.BlockSpec((tm, d), lambda i: (i, 0))
    return pl.pallas_call(
        _ffn_body,
        grid=(m // tm,),
        in_specs=[row, _resident((1, d)), _resident((d, ff)), _resident((d, ff)), _resident((ff, d))],
        out_specs=row,
        out_shape=jax.ShapeDtypeStruct((m, d), F32),
        compiler_params=_cparams(("parallel",), fuse_inputs=[False, False, True, True, True]),
        name="ffn",
    )(x2d, norm_w, wg, wu, wd)


def _mix_out_ffn(h2d, ya, yb, wo, norm_w, wg, wu, wd, final_w, final_norm):
    m, d = h2d.shape
    ff = wg.shape[1]
    tm = min(FFN_TM, m)
    row = pl.BlockSpec((tm, d), lambda i: (i, 0))
    return pl.pallas_call(
        functools.partial(_mix_out_ffn_body, final_norm=final_norm),
        grid=(m // tm,),
        in_specs=[row, row, row, _resident((d, d)), _resident((1, d)), _resident((d, ff)),
                  _resident((d, ff)), _resident((ff, d)), _resident((1, d))],
        out_specs=row,
        out_shape=jax.ShapeDtypeStruct((m, d), F32),
        compiler_params=_cparams(("parallel",),
                                 fuse_inputs=[False, False, False, True, False, True, True, True, False]),
        name="mix_out_ffn",
    )(h2d, ya, yb, wo, norm_w, wg, wu, wd, final_w)


def _proj_body(x_ref, nw_ref, w_ref, wst_ref, cw_ref, plain_ref, gate_ref, sm_ref, smt_ref, carry_ref, pe_ref):
    tm = x_ref.shape[0]
    n_plain, n_gate = plain_ref.shape[1], gate_ref.shape[1]
    wp_ref = w_ref.at[:, 0:n_plain]
    wa_ref = w_ref.at[:, n_plain:n_plain + n_gate]
    wb_ref = w_ref.at[:, n_plain + n_gate:n_plain + 2 * n_gate]
    ws_ref = w_ref.at[:, n_plain + 2 * n_gate:n_plain + 2 * n_gate + SMALL_W]
    n_gla = 2 * GLA_QK + GLA_V
    n_dn_qk = 2 * DN_QK

    @pl.when(pl.program_id(1) == 0)
    def _():
        carry_ref[...] = jnp.zeros_like(carry_ref)

    xn = _rms(x_ref[...], nw_ref[...]).astype(BF16)

    def plain_chunk(n0):
        cols = slice(n0, n0 + PROJ_TN)
        plain_ref[:, cols] = _dot(xn, wp_ref[:, cols]).astype(plain_ref.dtype)

    def gate_chunk(n0):
        cols = slice(n0, n0 + PROJ_TN)
        gate_ref[:, cols] = (_silu(_dot(xn, wa_ref[:, cols])) *
                             _sigmoid(_dot(xn, wb_ref[:, cols]))).astype(gate_ref.dtype)

    def conv_dot(ic):
        c0 = ic * CONV_TN
        ccols = slice(c0, c0 + CONV_TN)
        pe = pe_ref.at[ic % pe_ref.shape[0]]
        pe[0:CONV_HALO, 0:CONV_TN] = carry_ref[:, ccols]
        pe[CONV_HALO:, 0:CONV_TN] = _dot(xn, wp_ref[:, n_gla + c0:n_gla + c0 + CONV_TN])
        carry_ref[:, ccols] = pe[tm:tm + CONV_HALO, 0:CONV_TN]

    def conv_strips(ic):
        c0 = ic * CONV_TN
        ccols = slice(c0, c0 + CONV_TN)
        pe = pe_ref.at[ic % pe_ref.shape[0]]
        for r0 in range(0, tm, CONV_STRIP):
            win = pe[r0:r0 + CONV_STRIP + CONV_HALO, 0:CONV_TN]
            w0, w1, w2, w3 = (cw_ref[j:j + 1, ccols] for j in range(CONV_K))
            prev = pltpu.roll(win, 1, axis=0)
            y = (w3 * win + w2 * prev) + pltpu.roll(w1 * win + w0 * prev, 2, axis=0)
            y = _silu(y[CONV_HALO:, :])
            if c0 < n_dn_qk:
                scale = DN_DK ** -0.5 if c0 < DN_QK else 1.0
                heads = [y[:, h0:h0 + DN_DK] for h0 in range(0, CONV_TN, DN_DK)]
                y = jnp.concatenate(
                    [yh * (lax.rsqrt(jnp.sum(yh * yh, axis=-1, keepdims=True) + EPS) * scale)
                     for yh in heads], axis=1)
            plain_ref[r0:r0 + CONV_STRIP, n_gla + c0:n_gla + c0 + CONV_TN] = y.astype(plain_ref.dtype)
            if (r0 // CONV_STRIP) % CONV_PIECE == CONV_PIECE - 1:
                yield

    def mxu_pieces(tasks):
        for task in tasks:
            task()
            yield

    plain_tasks = [functools.partial(plain_chunk, n0) for n0 in range(0, n_gla, PROJ_TN)]
    gate_tasks = [functools.partial(gate_chunk, n0) for n0 in range(0, wa_ref.shape[1], PROJ_TN)]
    others = [t for pair in itertools.zip_longest(plain_tasks, gate_tasks) for t in pair if t is not None]
    n_conv = (wp_ref.shape[1] - n_gla) // CONV_TN
    conv_dot(0)
    for ic in range(n_conv):
        lo, hi = ic * len(others) // n_conv, (ic + 1) * len(others) // n_conv
        if ic + 1 < n_conv:
            conv_dot(ic + 1)
        _interleave(conv_strips(ic), mxu_pieces(others[lo:hi]))
    sm_ref[...] = _dot(xn, ws_ref[...])
    smt_ref[0] = _dot_nt(wst_ref[...], xn)


def _proj(h3d, norm_w, w_all, n_plain, n_gate, w_small_t, conv_w):
    b, t, d = h3d.shape
    tm = min(PROJ_TM, t)
    nt = t // tm
    n_conv = conv_w.shape[1]
    assert w_all.shape[1] == n_plain + 2 * n_gate + SMALL_W
    rows = lambda n: pl.BlockSpec((tm, n), lambda bi, ti: (bi * nt + ti, 0))
    return pl.pallas_call(
        _proj_body,
        grid=(b, nt),
        in_specs=[rows(d), _resident((1, d)), _resident(w_all.shape), _resident((SMALL_T_ROWS, d)),
                  _resident((CONV_K, n_conv))],
        out_specs=[rows(n_plain), rows(n_gate), rows(SMALL_W),
                   pl.BlockSpec((1, SMALL_T_ROWS, tm), lambda bi, ti: (bi, 0, ti))],
        out_shape=[
            jax.ShapeDtypeStruct((b * t, n_plain), BF16),
            jax.ShapeDtypeStruct((b * t, n_gate), BF16),
            jax.ShapeDtypeStruct((b * t, SMALL_W), F32),
            jax.ShapeDtypeStruct((b, SMALL_T_ROWS, t), F32),
        ],
        scratch_shapes=[pltpu.VMEM((CONV_HALO, n_conv), F32),
                        pltpu.VMEM((CONV_SLOTS, CONV_HALO + tm, CONV_TN + CONV_PAD), F32)],
        compiler_params=_cparams(("parallel", "arbitrary")),
        name="proj",
    )(h3d.reshape(b * t, d), norm_w, w_all, w_small_t, conv_w)


def _tri_masks():
    r = lax.broadcasted_iota(jnp.int32, (CHUNK, CHUNK), 0)
    c = lax.broadcasted_iota(jnp.int32, (CHUNK, CHUNK), 1)
    return r >= c, r > c, r <= c


def _gla_stream(qk_ref, v_ref, sm_ref, gate_ref, wgg_ref, bgg_ref, hnw_ref, o_ref, st_ref):
    causal, _, _ = _tri_masks()
    tt = qk_ref.shape[1]
    nc = tt // CHUNK
    r = lax.broadcasted_iota(jnp.int32, (tt, tt), 0)
    c = lax.broadcasted_iota(jnp.int32, (tt, tt), 1)
    bd_tril = jnp.where(((r // CHUNK) == (c // CHUNK)) & (r >= c), 1.0, 0.0).astype(F32)

    log_a = []
    for ci in range(nc):
        z = _dot(sm_ref[0, ci * CHUNK:(ci + 1) * CHUNK, :].astype(BF16), wgg_ref[...]) + bgg_ref[...]
        log_a.append((jnp.minimum(z, 0.0) - jnp.log1p(jnp.exp(-jnp.abs(z)))) * (1.0 / GLA_TAU))
        yield
    b = _ones_dot_rhs(bd_tril, jnp.concatenate(log_a, axis=0))
    hnw = hnw_ref[...]
    yield

    probs = [(ci, h) for ci in range(nc) for h in range(GLA_HEADS)]
    q_in, k_neg, k_state, a_chunk = [], [], [], []
    for ci in range(nc):
        rows = slice(ci * CHUNK, (ci + 1) * CHUNK)
        bc = b[rows]
        b_last = bc[CHUNK - 1:CHUNK, :]
        q = qk_ref[0, rows, 0:GLA_QK].astype(F32) * (GLA_DK ** -0.5)
        k = qk_ref[0, rows, GLA_QK:2 * GLA_QK].astype(F32)
        q_in.append((q * jnp.exp(bc)).astype(BF16))
        k_neg.append((k * jnp.exp(-bc)).astype(BF16))
        k_state.append((k * jnp.exp(b_last - bc)).astype(BF16))
        a_chunk.append(jnp.exp(b_last))
        yield
    kcols = lambda h: slice(h * GLA_DK, (h + 1) * GLA_DK)
    vcols = lambda h: slice(h * GLA_DV, (h + 1) * GLA_DV)
    vh = [v_ref[0, ci * CHUNK:(ci + 1) * CHUNK, vcols(h)] for ci, h in probs]
    scores, o_intra, d_st = [], [], []
    for v, (ci, h) in zip(vh, probs):
        s = _dot_nt(q_in[ci][:, kcols(h)], k_neg[ci][:, kcols(h)])
        scores.append(jnp.where(causal, s, 0.0).astype(BF16))
        d_st.append(_dot_tn(v, k_state[ci][:, kcols(h)]))
        if h % 2 == 1:
            yield
    for s, v, (ci, h) in zip(scores, vh, probs):
        o_intra.append(_dot(s, v))
        if h % 2 == 1:
            yield

    for i, (ci, h) in enumerate(probs):
        rows = slice(ci * CHUNK, (ci + 1) * CHUNK)
        st = st_ref[h]
        o = o_intra[i] + _dot_nt(q_in[ci][:, kcols(h)], st.astype(BF16))
        st_ref[h] = st * a_chunk[ci][:, kcols(h)] + d_st[i]
        o = o * lax.rsqrt(jnp.mean(o * o, axis=-1, keepdims=True) + EPS) * hnw
        o_ref[0, rows, vcols(h)] = (o * gate_ref[0, rows, vcols(h)].astype(F32)).astype(o_ref.dtype)
        if h % 2 == 1:
            yield


def _interleave(*streams):
    streams = list(streams)
    while streams:
        for g in list(streams):
            if next(g, StopIteration) is StopIteration:
                streams.remove(g)


def _gla_body(qk_ref, v_ref, sm_ref, gate_ref, wgg_ref, bgg_ref, hnw_ref, o_ref, st_ref):
    @pl.when(pl.program_id(1) == 0)
    def _():
        st_ref[...] = jnp.zeros_like(st_ref)

    _interleave(_gla_stream(qk_ref, v_ref, sm_ref, gate_ref, wgg_ref, bgg_ref, hnw_ref, o_ref, st_ref))


def _gla(plain, small, gates, wgg, bgg, hnw, b, t):
    tt = min(MIX_TT, t)
    return pl.pallas_call(
        _gla_body,
        grid=(b, t // tt),
        in_specs=[
            pl.BlockSpec((1, tt, 2 * GLA_QK), lambda bi, ti: (bi, ti, 0)),
            pl.BlockSpec((1, tt, GLA_V), lambda bi, ti: (bi, ti, 1)),
            pl.BlockSpec((1, tt, SMALL_W), lambda bi, ti: (bi, ti, 0)),
            pl.BlockSpec((1, tt, GLA_V), lambda bi, ti: (bi, ti, 0)),
            pl.BlockSpec((SMALL_W, GLA_QK), lambda bi, ti: (0, 0)),
            pl.BlockSpec((1, GLA_QK), lambda bi, ti: (0, 0)),
            pl.BlockSpec((1, GLA_DV), lambda bi, ti: (0, 0)),
        ],
        out_specs=pl.BlockSpec((1, tt, GLA_V), lambda bi, ti: (bi, ti, 0)),
        out_shape=jax.ShapeDtypeStruct((b, t, GLA_V), BF16),
        scratch_shapes=[pltpu.VMEM((GLA_HEADS, GLA_DV, GLA_DK), F32)],
        compiler_params=_cparams(("parallel", "arbitrary")),
        name="gla",
    )(plain, plain, small, gates, wgg, bgg, hnw)


def _gdn_body(q_ref, k_ref, v_ref, sm_ref, smt_ref, gate_ref, par_ref, part_ref, hnw_ref,
              o_ref, s_ref, u_ref, wq_ref, qk_ref, ks_ref, side_0=(), side_a=(), side_b=()):
    nb, tt = q_ref.shape[0], q_ref.shape[1]
    nc = tt // CHUNK
    heads = range(DN_HEADS)

    @pl.when(pl.program_id(1) == 0)
    def _():
        s_ref[...] = jnp.zeros_like(s_ref)

    causal, strict, _ = _tri_masks()
    r = lax.broadcasted_iota(jnp.int32, (tt, tt), 0)
    c = lax.broadcasted_iota(jnp.int32, (tt, tt), 1)
    same_chunk = (r // CHUNK) == (c // CHUNK)
    bd_tril = jnp.where(same_chunk & (r >= c), 1.0, 0.0).astype(F32)
    bd_triu = jnp.where(same_chunk & (r <= c), 1.0, 0.0).astype(F32)

    beta_all, g_col_all, g_row_all = [], [], []
    for bi in range(nb):
        sm = sm_ref[bi]
        beta_all.append(_sigmoid(sm))
        g_all = -jnp.exp(par_ref[0:1, :]) * _softplus(sm + par_ref[1:2, :])
        smt = smt_ref[bi]
        gt_all = -jnp.exp(part_ref[:, 0:1]) * _softplus(smt + part_ref[:, 1:2])
        g_col_all.append(_ones_dot_rhs(bd_tril, g_all))
        g_row_all.append(_ones_dot_lhs(gt_all, bd_triu))
    hnw = hnw_ref[...]

    ri = lax.broadcasted_iota(jnp.int32, (CHUNK, 2 * CHUNK), 0)
    li = lax.broadcasted_iota(jnp.int32, (CHUNK, 2 * CHUNK), 1)
    left = li < CHUNK
    eye2 = jnp.where((ri == li) | (ri + CHUNK == li), 1.0, 0.0).astype(F32)

    def phase_a(chunks):
        probs = [(bi, ci, h) for bi in range(nb) for ci in chunks for h in heads]
        kb, decay, exp_g, kq = [], [], [], []
        for bi, ci, h in probs:
            rows = slice(ci * CHUNK, (ci + 1) * CHUNK)
            q16 = q_ref[bi, rows, h * DN_DK:(h + 1) * DN_DK]
            k16 = k_ref[bi, rows, h * DN_DK:(h + 1) * DN_DK]
            k = k16.astype(F32)
            beta = beta_all[bi][rows, LANE_BETA + h:LANE_BETA + h + 1]
            g_col = g_col_all[bi][rows, LANE_DA + h:LANE_DA + h + 1]
            g_row = g_row_all[bi][LANE_DA + h:LANE_DA + h + 1, rows]
            g_last = g_row[:, CHUNK - 1:CHUNK]
            kb.append(k * beta)
            decay.append(jnp.exp(jnp.where(causal, g_col - g_row, -jnp.inf)))
            exp_g.append(jnp.exp(g_col))
            wq_ref[bi, ci, h, CHUNK:, :] = (q16.astype(F32) * exp_g[-1]).astype(BF16)
            ks_ref[bi, ci, h] = (k * jnp.exp(g_last - g_col)).astype(BF16)
            kq.append(_dot_nt(jnp.concatenate([kb[-1].astype(BF16), q16], axis=0), k16))
            if len(kq) % DN_HEADS == 0:
                yield
        def blockdiag(x):
            x16 = x.astype(BF16)
            zero = jnp.zeros_like(x16)
            return jnp.concatenate([jnp.where(left, x16, zero), jnp.where(left, zero, x16)], axis=0)

        neg_a = [-jnp.where(strict, x[0:CHUNK] * d, 0.0) for x, d in zip(kq, decay)]
        p = [jnp.concatenate([neg_a[i], neg_a[i + 1]], axis=1) for i in range(0, len(probs), 2)]
        inv = [eye2 + x for x in p]
        size = 2
        while size < CHUNK:
            p = [_dot(x.astype(BF16), blockdiag(x)) for x in p]
            yield
            inv = [t + _dot(t.astype(BF16), blockdiag(x)) for t, x in zip(inv, p)]
            yield
            size *= 2
        for i, (bi, ci, h) in enumerate(probs):
            rows = slice(ci * CHUNK, (ci + 1) * CHUNK)
            beta = beta_all[bi][rows, LANE_BETA + h:LANE_BETA + h + 1]
            vb = v_ref[bi, rows, h * DN_DV:(h + 1) * DN_DV].astype(F32) * beta
            rhs = jnp.concatenate([vb, kb[i] * exp_g[i]], axis=1).astype(BF16)
            pad = jnp.zeros_like(rhs)
            rhs2 = jnp.concatenate([rhs, pad] if i % 2 == 0 else [pad, rhs], axis=0)
            uw = _dot(inv[i // 2].astype(BF16), rhs2)
            u_ref[bi, ci, h] = uw[:, 0:DN_DV]
            wq_ref[bi, ci, h, 0:CHUNK, :] = uw[:, DN_DV:].astype(BF16)
            qk_ref[bi, ci, h] = (kq[i][CHUNK:] * decay[i]).astype(BF16)
        yield

    def phase_b(chunks):
        seqs = [(bi, h) for bi in range(nb) for h in heads]
        for ci in chunks:
            rows = slice(ci * CHUNK, (ci + 1) * CHUNK)
            last = ci * CHUNK + CHUNK - 1
            s = [s_ref[bi, h] for bi, h in seqs]
            ws = []
            for i, ((bi, h), x) in enumerate(zip(seqs, s)):
                ws.append(_dot(wq_ref[bi, ci, h], x.astype(BF16)))
                if i % GDN_YIELD == GDN_YIELD - 1:
                    yield
            vn16, o = [], []
            for i, ((bi, h), x, y) in enumerate(zip(seqs, s, ws)):
                vn = (u_ref[bi, ci, h] - y[0:CHUNK]).astype(BF16)
                vn16.append(vn)
                o.append(y[CHUNK:] + _dot(qk_ref[bi, ci, h], vn))
                g_chunk = jnp.exp(g_row_all[bi][LANE_DA + h:LANE_DA + h + 1, last:last + 1])
                s_ref[bi, h] = x * g_chunk + _dot_tn(ks_ref[bi, ci, h], vn)
                if i % GDN_YIELD == GDN_YIELD - 1:
                    yield
            for i, ((bi, h), x) in enumerate(zip(seqs, o)):
                cols = slice(h * DN_DV, (h + 1) * DN_DV)
                on = x * lax.rsqrt(jnp.mean(x * x, axis=-1, keepdims=True) + EPS) * hnw
                o_ref[bi, rows, cols] = (on * gate_ref[bi, rows, cols].astype(F32)).astype(o_ref.dtype)
                if i % GDN_YIELD == GDN_YIELD - 1:
                    yield

    pa = phase_a(range(nc))
    _interleave(itertools.islice(pa, nb * nc), *side_0)
    _interleave(pa, *side_a)
    _interleave(phase_b(range(nc)), *side_b)


def _gdn(plain, small, small_t, gates, par, par_t, hnw, b, t):
    tt = min(GDN_TT, t)
    nb = GDN_NB if b % GDN_NB == 0 else 1
    nc = tt // CHUNK
    return pl.pallas_call(
        _gdn_body,
        grid=(b // nb, t // tt),
        in_specs=[
            pl.BlockSpec((nb, tt, DN_QK), lambda bi, ti: (bi, ti, 2)),
            pl.BlockSpec((nb, tt, DN_QK), lambda bi, ti: (bi, ti, 3)),
            pl.BlockSpec((nb, tt, DN_V), lambda bi, ti: (bi, ti, 4)),
            pl.BlockSpec((nb, tt, SMALL_W), lambda bi, ti: (bi, ti, 0)),
            pl.BlockSpec((nb, SMALL_T_ROWS, tt), lambda bi, ti: (bi, 0, ti)),
            pl.BlockSpec((nb, tt, DN_V), lambda bi, ti: (bi, ti, 1)),
            pl.BlockSpec((2, SMALL_W), lambda bi, ti: (0, 0)),
            pl.BlockSpec((SMALL_T_ROWS, 2), lambda bi, ti: (0, 0)),
            pl.BlockSpec((1, DN_DV), lambda bi, ti: (0, 0)),
        ],
        out_specs=pl.BlockSpec((nb, tt, DN_V), lambda bi, ti: (bi, ti, 0)),
        out_shape=jax.ShapeDtypeStruct((b, t, DN_V), BF16),
        scratch_shapes=[
            pltpu.VMEM((nb, DN_HEADS, DN_DK, DN_DV), F32),
            pltpu.VMEM((nb, nc, DN_HEADS, CHUNK, DN_DV), F32),
            pltpu.VMEM((nb, nc, DN_HEADS, 2 * CHUNK, DN_DK), BF16),
            pltpu.VMEM((nb, nc, DN_HEADS, CHUNK, CHUNK), BF16),
            pltpu.VMEM((nb, nc, DN_HEADS, CHUNK, DN_DK), BF16),
        ],
        compiler_params=_cparams(("parallel", "arbitrary")),
        name="gdn",
    )(plain, plain, plain, small, small_t, gates, par, par_t, hnw)


def _mixers_body(*refs):
    (gla_qk, gla_v, gla_sm, gla_gate, wgg, bgg, gla_hnw,
     dn_q, dn_k, dn_v, dn_sm, dn_smt, dn_gate, par, part, dn_hnw,
     ya_ref, yb_ref, gla_st, dn_s, dn_u, dn_wq, dn_qk, dn_ks) = refs
    @pl.when(pl.program_id(1) == 0)
    def _():
        gla_st[...] = jnp.zeros_like(gla_st)

    gla = _gla_stream(gla_qk, gla_v, gla_sm, gla_gate, wgg, bgg, gla_hnw, ya_ref, gla_st)
    _gdn_body(dn_q, dn_k, dn_v, dn_sm, dn_smt, dn_gate, par, part, dn_hnw, yb_ref,
              dn_s, dn_u, dn_wq, dn_qk, dn_ks, side_b=(gla,))


def _mixers(plain, small, small_t, gates, wgg, bgg, gla_hnw, par, par_t, dn_hnw, b, t):
    tt = min(MIX_TT, t)
    nc = tt // CHUNK
    blk = lambda w, j: pl.BlockSpec((1, tt, w), lambda bi, ti: (bi, ti, j))
    const = lambda shape: pl.BlockSpec(shape, lambda bi, ti: (0,) * len(shape))
    return pl.pallas_call(
        _mixers_body,
        grid=(b, t // tt),
        in_specs=[
            blk(2 * GLA_QK, 0), blk(GLA_V, 1), blk(SMALL_W, 0), blk(GLA_V, 0),
            const((SMALL_W, GLA_QK)), const((1, GLA_QK)), const((1, GLA_DV)),
            blk(DN_QK, 2), blk(DN_QK, 3), blk(DN_V, 4), blk(SMALL_W, 0),
            pl.BlockSpec((1, SMALL_T_ROWS, tt), lambda bi, ti: (bi, 0, ti)),
            blk(DN_V, 1),
            const((2, SMALL_W)), const((SMALL_T_ROWS, 2)), const((1, DN_DV)),
        ],
        out_specs=[blk(GLA_V, 0), blk(DN_V, 0)],
        out_shape=[jax.ShapeDtypeStruct((b, t, GLA_V), BF16), jax.ShapeDtypeStruct((b, t, DN_V), BF16)],
        scratch_shapes=[
            pltpu.VMEM((GLA_HEADS, GLA_DV, GLA_DK), F32),
            pltpu.VMEM((1, DN_HEADS, DN_DK, DN_DV), F32),
            pltpu.VMEM((1, nc, DN_HEADS, CHUNK, DN_DV), F32),
            pltpu.VMEM((1, nc, DN_HEADS, 2 * CHUNK, DN_DK), BF16),
            pltpu.VMEM((1, nc, DN_HEADS, CHUNK, CHUNK), BF16),
            pltpu.VMEM((1, nc, DN_HEADS, CHUNK, DN_DK), BF16),
        ],
        compiler_params=_cparams(("parallel", "arbitrary")),
        name="mixers",
    )(plain, plain, small, gates, wgg, bgg, gla_hnw,
      plain, plain, plain, small, small_t, gates, par, par_t, dn_hnw)


def _layer(h, ffn1_norm, ffn1_w_gate, ffn1_w_up, ffn1_w_down, mix_norm, w_in, w_gla_gate, b_gla_gate,
           conv_w, dn_a_log, dn_dt_bias, gla_head_norm, dn_head_norm, w_out, ffn2_norm, ffn2_w_gate,
           ffn2_w_up, ffn2_w_down, final_w, final_norm):
    b, t, d = h.shape
    m = b * t
    row = lambda a: a.reshape(1, -1).astype(F32)

    h1 = _ffn(h.reshape(m, d), row(ffn1_norm), ffn1_w_gate.astype(BF16), ffn1_w_up.astype(BF16),
              ffn1_w_down.astype(BF16))

    sizes = (GLA_QK, GLA_QK, GLA_V, GLA_V, GLA_RANK, DN_QK, DN_QK, DN_V, DN_V, DN_HEADS, DN_HEADS,
             D_MODEL, D_MODEL)
    offs = [0]
    for s in sizes:
        offs.append(offs[-1] + s)
    col = lambda i: w_in[:, offs[i]:offs[i + 1]]
    (w_gq, w_gk, w_gv, w_gr, w_glr, w_dq, w_dk, w_dv, w_dgate, w_dbeta, w_da, w_ma, w_mb) = (
        col(i) for i in range(len(sizes)))
    n_plain, n_gate = 2 * GLA_QK + GLA_V + 2 * DN_QK + DN_V, GLA_V + DN_V
    w_all = jnp.concatenate(
        [w_gq, w_gk, w_gv, w_dq, w_dk, w_dv, w_gr, w_dgate, w_ma, w_mb, w_glr, w_dbeta, w_da,
         jnp.zeros((d, SMALL_W - LANE_DA - DN_HEADS), F32)], axis=1).astype(BF16)
    w_small_t = w_all[:, n_plain + 2 * n_gate:n_plain + 2 * n_gate + SMALL_T_ROWS].T

    mixn = row(mix_norm)
    plain, gates, small, small_t = _proj(h1.reshape(b, t, d), mixn, w_all, n_plain, n_gate, w_small_t,
                                         conv_w.astype(F32))
    plain, gates, small = (a.reshape(b, t, -1) for a in (plain, gates, small))

    wgg = jnp.zeros((SMALL_W, GLA_QK), F32).at[LANE_GLR:LANE_GLR + GLA_RANK].set(w_gla_gate).astype(BF16)
    par = jnp.zeros((2, SMALL_W), F32)
    par = par.at[0, LANE_DA:LANE_DA + DN_HEADS].set(dn_a_log.astype(F32))
    par = par.at[1, LANE_DA:LANE_DA + DN_HEADS].set(dn_dt_bias.astype(F32))
    par_t = par[:, :SMALL_T_ROWS].T
    ya, yb = _mixers(plain, small, small_t, gates, wgg, row(b_gla_gate), row(gla_head_norm), par, par_t,
                     row(dn_head_norm), b, t)

    h3 = _mix_out_ffn(h1, ya.reshape(m, d), yb.reshape(m, d), w_out.astype(BF16), row(ffn2_norm),
                      ffn2_w_gate.astype(BF16), ffn2_w_up.astype(BF16), ffn2_w_down.astype(BF16),
                      row(final_w), final_norm)
    return h3.reshape(b, t, d)


def kernel(x, ffn1_norm, ffn1_w_gate, ffn1_w_up, ffn1_w_down, mix_norm, w_in, w_gla_gate, b_gla_gate,
           conv_w, dn_a_log, dn_dt_bias, gla_head_norm, dn_head_norm, w_out, ffn2_norm, ffn2_w_gate,
           ffn2_w_up, ffn2_w_down, final_norm):
    depth = ffn1_norm.shape[0]
    h = x
    for layer in range(depth):
        h = _layer(h, ffn1_norm[layer], ffn1_w_gate[layer], ffn1_w_up[layer], ffn1_w_down[layer],
                   mix_norm[layer], w_in[layer], w_gla_gate[layer], b_gla_gate[layer], conv_w[layer],
                   dn_a_log[layer], dn_dt_bias[layer], gla_head_norm[layer], dn_head_norm[layer],
                   w_out[layer], ffn2_norm[layer], ffn2_w_gate[layer], ffn2_w_up[layer],
                   ffn2_w_down[layer], final_norm, layer == depth - 1)
    return h
```
